```python
import jax, jax.numpy as jnp
from jax import lax
import numpy as np

D_MODEL = 1024
BATCH = 16
SEQ = 4096
DEPTH = 4

CHUNK = 64
Q_BLOCK = 128
HEAD_DIM = 64
SB_HEADS = 8
FOX_HEADS = 8
SB_WIDTH = SB_HEADS * HEAD_DIM
FOX_WIDTH = FOX_HEADS * HEAD_DIM
SGU_GROUPS = 8
SGU_WIDTH = 512
SGU_CHUNK = 128
N_BRANCH = 3
BRANCH_WIDTH = 512
D_FF = 2816
CONV_WIDTH = 3
EPS = 1e-6

SPLIT_SIZES = (SB_WIDTH, SB_WIDTH, SB_WIDTH,
               FOX_WIDTH, FOX_WIDTH, FOX_WIDTH,
               FOX_HEADS,
               2 * SGU_WIDTH,
               N_BRANCH * D_MODEL)
IN_COLS = 3 * SB_WIDTH + 3 * FOX_WIDTH + FOX_HEADS + 2 * SGU_WIDTH + N_BRANCH * D_MODEL

kernel_name = "hybrid_sb_fox_sgu_gated_trunk"


def rmsnorm(x, g):
    xf = x.astype(jnp.float32)
    y = xf * lax.rsqrt(jnp.mean(xf * xf, axis=-1, keepdims=True) + EPS)
    return (y * g.astype(jnp.float32)).astype(x.dtype)


def split_heads(t, n_heads):
    b, s, _ = t.shape
    return t.reshape(b, s, n_heads, HEAD_DIM).transpose(0, 2, 1, 3)


def merge_heads(t):
    b, h, s, d = t.shape
    return t.transpose(0, 2, 1, 3).reshape(b, s, h * d)


def stick_breaking_attention(q, k, v):
    seq = q.shape[2]
    scale = HEAD_DIM ** -0.5
    outs = []
    for start in range(0, seq, Q_BLOCK):
        end = start + Q_BLOCK
        qb = q[:, :, start:end].astype(jnp.float32)
        kb = k[:, :, :end].astype(jnp.float32)
        vb = v[:, :, :end]
        z = jnp.einsum('bhqd,bhkd->bhqk', qb, kb) * scale
        t_pos = jnp.arange(start, end)[:, None]
        s_pos = jnp.arange(end)[None, :]
        strict = s_pos < t_pos
        log_stay = jnp.where(strict, jax.nn.log_sigmoid(-z), 0.0)
        later = lax.cumsum(log_stay, axis=3, reverse=True) - log_stay
        w = jnp.where(strict, jnp.exp(jax.nn.log_sigmoid(z) + later), 0.0)
        outs.append(jnp.einsum('bhqk,bhkd->bhqd', w.astype(vb.dtype), vb))
    return jnp.concatenate(outs, axis=2)


def forgetting_attention(q, k, v, log_f):
    seq = q.shape[2]
    scale = HEAD_DIM ** -0.5
    c = jnp.cumsum(log_f, axis=-1)
    outs = []
    for start in range(0, seq, Q_BLOCK):
        end = start + Q_BLOCK
        qb = q[:, :, start:end].astype(jnp.float32)
        kb = k[:, :, :end].astype(jnp.float32)
        vb = v[:, :, :end]
        z = (jnp.einsum('bhqd,bhkd->bhqk', qb, kb) * scale
             + c[:, :, start:end, None] - c[:, :, None, :end])
        causal = jnp.arange(end)[None, :] <= jnp.arange(start, end)[:, None]
        p = jax.nn.softmax(jnp.where(causal, z, -jnp.inf), axis=-1)
        outs.append(jnp.einsum('bhqk,bhkd->bhqd', p.astype(vb.dtype), vb))
    return jnp.concatenate(outs, axis=2)


def spatial_gating(u, v, ln_g, w_s, b_s):
    b, s, _ = v.shape
    vf = v.astype(jnp.float32)
    mu = jnp.mean(vf, axis=-1, keepdims=True)
    var = jnp.mean(jnp.square(vf - mu), axis=-1, keepdims=True)
    vn = (vf - mu) * lax.rsqrt(var + EPS) * ln_g.astype(jnp.float32)
    vc = vn.reshape(b, s // SGU_CHUNK, SGU_CHUNK, SGU_GROUPS, SGU_WIDTH // SGU_GROUPS)
    pos = jnp.arange(SGU_CHUNK)
    mask = (pos[None, :] // CHUNK) <= (pos[:, None] // CHUNK)
    w = jnp.where(mask[None], w_s, 0.0).astype(jnp.float32)
    mixed = (jnp.einsum('gts,bnsgc->bntgc', w, vc)
             + b_s.astype(jnp.float32).T[None, None, :, :, None])
    return (u.astype(jnp.float32) * mixed.reshape(b, s, SGU_WIDTH)).astype(u.dtype)


def conv_glu_ffn(x, w_up, conv_w, conv_b, w_down):
    s = x.shape[1]
    gate, value = jnp.split(x @ w_up, 2, axis=-1)
    gp = jnp.pad(gate, ((0, 0), (CONV_WIDTH - 1, 0), (0, 0)))
    conv = conv_b + gp[:, 0:s] * conv_w[0]
    for i in range(1, CONV_WIDTH):
        conv = conv + gp[:, i:i + s] * conv_w[i]
    return (jax.nn.gelu(conv, approximate=False) * value) @ w_down


def setup_inputs(seed: int = 0) -> dict:
    key = jax.random.key(seed)
    ks = jax.random.split(key, 16)

    def nrm(k, shape, scale):
        return jax.random.normal(k, shape, jnp.float32) * scale

    x = nrm(ks[0], (BATCH, SEQ, D_MODEL), 1.0)
    norm_mix_g = 1.0 + nrm(ks[1], (DEPTH, D_MODEL), 0.1)
    w_in = nrm(ks[2], (DEPTH, D_MODEL, IN_COLS), D_MODEL ** -0.5)
    fox_bias = jnp.linspace(1.0, 4.0, FOX_HEADS, dtype=jnp.float32)[None, :] + nrm(ks[3], (DEPTH, FOX_HEADS), 0.1)
    sgu_ln_g = 1.0 + nrm(ks[4], (DEPTH, SGU_WIDTH), 0.1)
    sgu_w = nrm(ks[5], (DEPTH, SGU_GROUPS, SGU_CHUNK, SGU_CHUNK), SGU_CHUNK ** -0.5)
    sgu_b = 1.0 + nrm(ks[6], (DEPTH, SGU_GROUPS, SGU_CHUNK), 0.1)
    w_branch = nrm(ks[7], (DEPTH, N_BRANCH, BRANCH_WIDTH, D_MODEL), BRANCH_WIDTH ** -0.5)
    w_out = nrm(ks[8], (DEPTH, D_MODEL, D_MODEL), D_MODEL ** -0.5)
    norm_ffn_g = 1.0 + nrm(ks[9], (DEPTH, D_MODEL), 0.1)
    w_up = nrm(ks[10], (DEPTH, D_MODEL, 2 * D_FF), D_MODEL ** -0.5)
    conv_w = nrm(ks[11], (DEPTH, CONV_WIDTH, D_FF), CONV_WIDTH ** -0.5)
    conv_b = nrm(ks[12], (DEPTH, D_FF), 0.02)
    w_down = nrm(ks[13], (DEPTH, D_FF, D_MODEL), D_FF ** -0.5)
    final_g = 1.0 + nrm(ks[14], (D_MODEL,), 0.1)
    return {"x": x, "norm_mix_g": norm_mix_g, "w_in": w_in, "fox_bias": fox_bias,
            "sgu_ln_g": sgu_ln_g, "sgu_w": sgu_w, "sgu_b": sgu_b, "w_branch": w_branch,
            "w_out": w_out, "norm_ffn_g": norm_ffn_g, "w_up": w_up, "conv_w": conv_w,
            "conv_b": conv_b, "w_down": w_down, "final_g": final_g}


def reference(x, norm_mix_g, w_in, fox_bias, sgu_ln_g, sgu_w, sgu_b, w_branch,
              w_out, norm_ffn_g, w_up, conv_w, conv_b, w_down, final_g):
    b, s, _ = x.shape
    offsets = np.cumsum(SPLIT_SIZES)[:-1].tolist()
    h = x
    for l in range(DEPTH):
        xn = rmsnorm(h, norm_mix_g[l])
        proj = xn @ w_in[l]
        q_a, k_a, v_a, q_b, k_b, v_b, f_b, uv_c, gates = jnp.split(proj, offsets, axis=-1)

        y_a = merge_heads(stick_breaking_attention(
            split_heads(q_a, SB_HEADS), split_heads(k_a, SB_HEADS), split_heads(v_a, SB_HEADS)))

        log_f = jax.nn.log_sigmoid(f_b.astype(jnp.float32) + fox_bias[l].astype(jnp.float32))
        y_b = merge_heads(forgetting_attention(
            split_heads(q_b, FOX_HEADS), split_heads(k_b, FOX_HEADS), split_heads(v_b, FOX_HEADS),
            log_f.transpose(0, 2, 1)))

        u_c, v_c = jnp.split(jax.nn.gelu(uv_c, approximate=False), 2, axis=-1)
        y_c = spatial_gating(u_c, v_c, sgu_ln_g[l], sgu_w[l], sgu_b[l])

        g = jax.nn.sigmoid(gates.reshape(b, s, N_BRANCH, D_MODEL))
        merged = (g[:, :, 0] * (y_a @ w_branch[l, 0])
                  + g[:, :, 1] * (y_b @ w_branch[l, 1])
                  + g[:, :, 2] * (y_c @ w_branch[l, 2]))
        h = h + merged @ w_out[l]

        h = h + conv_glu_ffn(rmsnorm(h, norm_ffn_g[l]), w_up[l], conv_w[l], conv_b[l], w_down[l])
    return rmsnorm(h, final_g)
```

```python
import functools

import jax
import jax.numpy as jnp
from jax import lax
from jax.experimental import pallas as pl
from jax.experimental.pallas import tpu as pltpu

F32 = jnp.float32
BF16 = jnp.bfloat16

D_MODEL = 1024
DEPTH = 4
HEAD_DIM = 64
N_HEADS = 8
ATT_WIDTH = N_HEADS * HEAD_DIM
SGU_WIDTH = 512
SGU_GROUPS = 8
SGU_CHUNK = 128
STREAM_CHUNK = 64
N_BRANCH = 3
D_FF = 2816
EPS = 1e-6

LANES = 128
HEAD_PAIRS = ATT_WIDTH // LANES
QKV_COLS = 6 * ATT_WIDTH
UV_COLS = 2 * SGU_WIDTH
GATE_COLS = N_BRANCH * D_MODEL
MAIN_COLS = QKV_COLS + UV_COLS + GATE_COLS
GATE_BLOCK = QKV_COLS // GATE_COLS
UV_BLOCK = (QKV_COLS + GATE_COLS) // UV_COLS
assert GATE_BLOCK * GATE_COLS == QKV_COLS and UV_BLOCK * UV_COLS == QKV_COLS + GATE_COLS

VMEM_LIMIT = 56 * 1024 * 1024

PROJ_TM = 1024
PROJ_TN = 1024
SGU_TM = 512
ATT_BQ = 128
ATT_BK = 128
MERGE_TM = 512
FFN_TM = 512
FFN_TF = 1408
NEG_BIG = -1e30


def _rms(x, g):
    ms = jnp.mean(x * x, axis=-1, keepdims=True)
    return x * lax.rsqrt(ms + EPS) * g


def _gelu(x):
    return 0.5 * x * (1.0 + lax.erf(x * (2.0 ** -0.5)))


def _softplus(z):
    return jnp.maximum(z, 0.0) + jnp.log(1.0 + jnp.exp(-jnp.abs(z)))


def _dot(a, b):
    return jnp.dot(a, b, preferred_element_type=F32)


def _dot_nt(a, b):
    return lax.dot_general(a, b, (((1,), (1,)), ((), ())), preferred_element_type=F32)


def _proj_kernel(x_ref, g_ref, w_ref, wf_ref, o_ref, f_ref, xn_ref):
    @pl.when(pl.program_id(1) == 0)
    def _():
        xn = _rms(x_ref[...], g_ref[...]).astype(BF16)
        xn_ref[...] = xn
        f_ref[...] = _dot(xn, wf_ref[...])

    o_ref[...] = _dot(xn_ref[...], w_ref[...]).astype(o_ref.dtype)


def _proj(h, g, w_main, w_f):
    t = h.shape[0]
    grid = (t // PROJ_TM, MAIN_COLS // PROJ_TN)
    return pl.pallas_call(
        _proj_kernel,
        grid=grid,
        in_specs=[
            pl.BlockSpec((PROJ_TM, D_MODEL), lambda i, j: (i, 0)),
            pl.BlockSpec((1, D_MODEL), lambda i, j: (0, 0)),
            pl.BlockSpec((D_MODEL, PROJ_TN), lambda i, j: (0, j)),
            pl.BlockSpec((D_MODEL, LANES), lambda i, j: (0, 0)),
        ],
        out_specs=[
            pl.BlockSpec((PROJ_TM, PROJ_TN), lambda i, j: (i, j)),
            pl.BlockSpec((PROJ_TM, LANES), lambda i, j: (i, 0)),
        ],
        out_shape=[
            jax.ShapeDtypeStruct((t, MAIN_COLS), BF16),
            jax.ShapeDtypeStruct((t, LANES), F32),
        ],
        scratch_shapes=[pltpu.VMEM((PROJ_TM, D_MODEL), BF16)],
        compiler_params=pltpu.CompilerParams(
            dimension_semantics=("arbitrary", "arbitrary"),
            vmem_limit_bytes=VMEM_LIMIT),
        name="proj",
    )(h, g, w_main, w_f)


def _fgate_kernel(f_ref, b_ref, c_ref, *, seq):
    r = lax.broadcasted_iota(jnp.int32, (LANES, LANES), 0)
    c = lax.broadcasted_iota(jnp.int32, (LANES, LANES), 1)
    lower = (c <= r).astype(F32)
    bias = b_ref[...]

    def body(i, carry):
        rows = pl.ds(pl.multiple_of(i * LANES, LANES), LANES)
        x = f_ref[rows, :] + bias
        log_f = -_softplus(-x)
        cs = jnp.dot(lower, log_f, precision=lax.Precision.HIGHEST,
                     preferred_element_type=F32) + carry
        c_ref[rows, :] = cs
        return cs[LANES - 1:LANES, :]

    lax.fori_loop(0, seq // LANES, body, jnp.zeros((1, LANES), F32))


def _fgate(f, bias, batch, seq):
    return pl.pallas_call(
        functools.partial(_fgate_kernel, seq=seq),
        grid=(batch,),
        in_specs=[
            pl.BlockSpec((seq, LANES), lambda b: (b, 0)),
            pl.BlockSpec((1, LANES), lambda b: (0, 0)),
        ],
        out_specs=pl.BlockSpec((seq, LANES), lambda b: (b, 0)),
        out_shape=jax.ShapeDtypeStruct(f.shape, F32),
        compiler_params=pltpu.CompilerParams(
            dimension_semantics=("arbitrary",), vmem_limit_bytes=VMEM_LIMIT),
        name="fgate",
    )(f, bias)


def _sgu_kernel(uv_ref, lng_ref, ws_ref, b_ref, o_ref):
    uv = _gelu(uv_ref[...].astype(F32))
    u = uv[:, :SGU_WIDTH]
    v = uv[:, SGU_WIDTH:]
    mu = jnp.mean(v, axis=-1, keepdims=True)
    vc = v - mu
    var = jnp.mean(vc * vc, axis=-1, keepdims=True)
    vn = (vc * lax.rsqrt(var + EPS) * lng_ref[...]).astype(BF16)

    t_pos = lax.broadcasted_iota(jnp.int32, (SGU_CHUNK, SGU_CHUNK), 0)
    s_pos = lax.broadcasted_iota(jnp.int32, (SGU_CHUNK, SGU_CHUNK), 1)
    mask = (s_pos // STREAM_CHUNK) <= (t_pos // STREAM_CHUNK)
    lane = lax.broadcasted_iota(jnp.int32, (SGU_CHUNK, LANES), 1)
    first_group = lane < HEAD_DIM
    group_width = SGU_WIDTH // SGU_GROUPS
    assert 2 * group_width == LANES

    for p in range(SGU_GROUPS // 2):
        cols = slice(p * LANES, (p + 1) * LANES)
        w0 = jnp.where(mask, ws_ref[2 * p], 0.0).astype(BF16)
        w1 = jnp.where(mask, ws_ref[2 * p + 1], 0.0).astype(BF16)
        bias = b_ref[:, cols]
        for c in range(SGU_TM // SGU_CHUNK):
            rows = slice(c * SGU_CHUNK, (c + 1) * SGU_CHUNK)
            vp = vn[rows, cols]
            mixed = jnp.where(first_group, _dot(w0, vp), _dot(w1, vp)) + bias
            o_ref[rows, cols] = (u[rows, cols] * mixed).astype(o_ref.dtype)


def _sgu(proj, ln_g, w_s, b_full):
    t = proj.shape[0]
    return pl.pallas_call(
        _sgu_kernel,
        grid=(t // SGU_TM,),
        in_specs=[
            pl.BlockSpec((SGU_TM, UV_COLS), lambda i: (i, UV_BLOCK)),
            pl.BlockSpec((1, SGU_WIDTH), lambda i: (0, 0)),
            pl.BlockSpec((SGU_GROUPS, SGU_CHUNK, SGU_CHUNK), lambda i: (0, 0, 0)),
            pl.BlockSpec((SGU_CHUNK, SGU_WIDTH), lambda i: (0, 0)),
        ],
        out_specs=pl.BlockSpec((SGU_TM, SGU_WIDTH), lambda i: (i, 0)),
        out_shape=jax.ShapeDtypeStruct((t, SGU_WIDTH), BF16),
        compiler_params=pltpu.CompilerParams(
            dimension_semantics=("arbitrary",), vmem_limit_bytes=VMEM_LIMIT),
        name="sgu",
    )(proj, ln_g, w_s, b_full)


def _sb_kernel(q_ref, k_ref, v_ref, o_ref):
    qi = pl.program_id(2)
    lane = lax.broadcasted_iota(jnp.int32, (ATT_BQ, LANES), 1)
    first_head = lane < HEAD_DIM
    q = q_ref[...] * (HEAD_DIM ** -0.5)
    row = lax.broadcasted_iota(jnp.int32, (ATT_BQ, ATT_BK), 0)
    col = lax.broadcasted_iota(jnp.int32, (ATT_BQ, ATT_BK), 1)
    strict = col < row
    neg_later = jnp.where(row > col, -1.0, 0.0).astype(BF16)

    def head(qm):
        def block(kb, carry, diag):
            stay, y = carry
            rows = pl.ds(pl.multiple_of(kb * ATT_BK, ATT_BK), ATT_BK)
            k = k_ref[rows, :]
            v = v_ref[rows, :]
            z = _dot_nt(qm, k)
            sp = _softplus(z)
            if diag:
                sp = jnp.where(strict, sp, 0.0)
            hi = sp.astype(BF16)
            lo = (sp - hi.astype(F32)).astype(BF16)
            later = _dot(hi, neg_later) + _dot(lo, neg_later)
            w = jnp.exp(z - sp + later)
            if diag:
                w = jnp.where(strict, w, 0.0)
            pv = _dot(w.astype(BF16), v)
            y = y + jnp.exp(stay) * pv
            stay = stay + (later[:, 0:1] - sp[:, 0:1])
            return stay, y

        carry = block(qi, (jnp.zeros((ATT_BQ, 1), F32), jnp.zeros((ATT_BQ, LANES), F32)), True)
        carry = lax.fori_loop(0, qi, lambda i, c: block(qi - 1 - i, c, False), carry)
        return carry[1]

    zero = jnp.zeros_like(q)
    y0 = head(jnp.where(first_head, q, zero))
    y1 = head(jnp.where(first_head, zero, q))
    o_ref[...] = jnp.where(first_head, y0, y1).astype(o_ref.dtype)


def _sb_attention(proj, batch, seq):
    nq = seq // ATT_BQ
    return pl.pallas_call(
        _sb_kernel,
        grid=(batch, HEAD_PAIRS, nq),
        in_specs=[
            pl.BlockSpec((ATT_BQ, LANES), lambda b, p, i: (b * nq + i, p)),
            pl.BlockSpec((seq, LANES), lambda b, p, i: (b, HEAD_PAIRS + p)),
            pl.BlockSpec((seq, LANES), lambda b, p, i: (b, 2 * HEAD_PAIRS + p)),
        ],
        out_specs=pl.BlockSpec((ATT_BQ, LANES), lambda b, p, i: (b * nq + i, p)),
        out_shape=jax.ShapeDtypeStruct((batch * seq, ATT_WIDTH), BF16),
        compiler_params=pltpu.CompilerParams(
            dimension_semantics=("arbitrary", "arbitrary", "arbitrary"),
            vmem_limit_bytes=VMEM_LIMIT),
        name="sb_attention",
    )(proj, proj, proj)


def _fox_kernel(q_ref, k_ref, v_ref, ccol_ref, crow0_ref, crow1_ref, o_ref):
    hp = pl.program_id(1)
    qi = pl.program_id(2)
    lane = lax.broadcasted_iota(jnp.int32, (ATT_BQ, LANES), 1)
    first_head = lane < HEAD_DIM
    q = q_ref[...] * (HEAD_DIM ** -0.5)
    row = lax.broadcasted_iota(jnp.int32, (ATT_BQ, ATT_BK), 0)
    col = lax.broadcasted_iota(jnp.int32, (ATT_BQ, ATT_BK), 1)
    causal = col <= row
    ccol = ccol_ref[...]

    def head(qm, h, crow_ref):
        c_t = jnp.sum(jnp.where(lane == h, ccol, 0.0), axis=1, keepdims=True)

        def block(kb, carry, diag):
            m, l, acc = carry
            rows = pl.ds(pl.multiple_of(kb * ATT_BK, ATT_BK), ATT_BK)
            k = k_ref[rows, :]
            v = v_ref[rows, :]
            c_s = crow_ref[0, 0, :, rows]
            z = _dot_nt(qm, k) + (c_t - c_s)
            if diag:
                z = jnp.where(causal, z, NEG_BIG)
            m_new = jnp.maximum(m, jnp.max(z, axis=1, keepdims=True))
            alpha = jnp.exp(m - m_new)
            p = jnp.exp(z - m_new)
            l = alpha * l + jnp.sum(p, axis=1, keepdims=True)
            acc = alpha * acc + _dot(p.astype(BF16), v)
            return m_new, l, acc

        init = (jnp.full((ATT_BQ, 1), NEG_BIG, F32), jnp.zeros((ATT_BQ, 1), F32),
                jnp.zeros((ATT_BQ, LANES), F32))
        carry = block(qi, init, True)
        m, l, acc = lax.fori_loop(0, qi, lambda i, c: block(qi - 1 - i, c, False), carry)
        return acc / l

    zero = jnp.zeros_like(q)
    y0 = head(jnp.where(first_head, q, zero), 2 * hp, crow0_ref)
    y1 = head(jnp.where(first_head, zero, q), 2 * hp + 1, crow1_ref)
    o_ref[...] = jnp.where(first_head, y0, y1).astype(o_ref.dtype)


def _fox_attention(proj, c_col, c_row, batch, seq):
    nq = seq // ATT_BQ
    return pl.pallas_call(
        _fox_kernel,
        grid=(batch, HEAD_PAIRS, nq),
        in_specs=[
            pl.BlockSpec((ATT_BQ, LANES), lambda b, p, i: (b * nq + i, 3 * HEAD_PAIRS + p)),
            pl.BlockSpec((seq, LANES), lambda b, p, i: (b, 4 * HEAD_PAIRS + p)),
            pl.BlockSpec((seq, LANES), lambda b, p, i: (b, 5 * HEAD_PAIRS + p)),
            pl.BlockSpec((ATT_BQ, LANES), lambda b, p, i: (b * nq + i, 0)),
            pl.BlockSpec((1, 1, 1, seq), lambda b, p, i: (b, 2 * p, 0, 0)),
            pl.BlockSpec((1, 1, 1, seq), lambda b, p, i: (b, 2 * p + 1, 0, 0)),
        ],
        out_specs=pl.BlockSpec((ATT_BQ, LANES), lambda b, p, i: (b * nq + i, p)),
        out_shape=jax.ShapeDtypeStruct((batch * seq, ATT_WIDTH), BF16),
        compiler_params=pltpu.CompilerParams(
            dimension_semantics=("arbitrary", "arbitrary", "arbitrary"),
            vmem_limit_bytes=VMEM_LIMIT),
        name="fox_attention",
    )(proj, proj, proj, c_col, c_row, c_row)


def _merge_kernel(ya_ref, yb_ref, yc_ref, gate_ref, h_ref, wbr_ref, wout_ref, g_ref,
                  h_out_ref, xn_out_ref):
    merged = None
    for i, y_ref in enumerate((ya_ref, yb_ref, yc_ref)):
        gate = gate_ref[:, i * D_MODEL:(i + 1) * D_MODEL].astype(F32)
        term = (1.0 / (1.0 + jnp.exp(-gate))) * _dot(y_ref[...], wbr_ref[i])
        merged = term if merged is None else merged + term
    h_new = h_ref[...] + _dot(merged.astype(BF16), wout_ref[...])
    h_out_ref[...] = h_new
    xn_out_ref[...] = _rms(h_new, g_ref[...]).astype(xn_out_ref.dtype)


def _merge(y_a, y_b, y_c, proj, h, w_br, w_out, g_ffn):
    t = h.shape[0]
    y_spec = pl.BlockSpec((MERGE_TM, ATT_WIDTH), lambda i: (i, 0))
    return pl.pallas_call(
        _merge_kernel,
        grid=(t // MERGE_TM,),
        in_specs=[
            y_spec, y_spec, y_spec,
            pl.BlockSpec((MERGE_TM, GATE_COLS), lambda i: (i, GATE_BLOCK)),
            pl.BlockSpec((MERGE_TM, D_MODEL), lambda i: (i, 0)),
            pl.BlockSpec((N_BRANCH, ATT_WIDTH, D_MODEL), lambda i: (0, 0, 0)),
            pl.BlockSpec((D_MODEL, D_MODEL), lambda i: (0, 0)),
            pl.BlockSpec((1, D_MODEL), lambda i: (0, 0)),
        ],
        out_specs=[
            pl.BlockSpec((MERGE_TM, D_MODEL), lambda i: (i, 0)),
            pl.BlockSpec((MERGE_TM, D_MODEL), lambda i: (i, 0)),
        ],
        out_shape=[
            jax.ShapeDtypeStruct((t, D_MODEL), F32),
            jax.ShapeDtypeStruct((t, D_MODEL), BF16),
        ],
        compiler_params=pltpu.CompilerParams(
            dimension_semantics=("arbitrary",), vmem_limit_bytes=VMEM_LIMIT),
        name="merge",
    )(y_a, y_b, y_c, proj, h, w_br, w_out, g_ffn)


CONV_HALO = 8


def _ffn_kernel(xn_ref, h_ref, wg_ref, wv_ref, cw_ref, cb_ref, wd_ref, fg_ref, o_ref,
                acc_ref, gbuf_ref, carry_ref, *, n_f, final):
    si = pl.program_id(1)
    f = pl.program_id(2)
    x = xn_ref[...]
    gate = _dot(x, wg_ref[...])
    value = _dot(x, wv_ref[...])

    @pl.when(si == 0)
    def _():
        carry_ref[f] = jnp.zeros((CONV_HALO, FFN_TF), F32)

    gbuf_ref[0:CONV_HALO, :] = carry_ref[f]
    gbuf_ref[CONV_HALO:CONV_HALO + FFN_TM, :] = gate
    carry_ref[f] = gate[FFN_TM - CONV_HALO:FFN_TM, :]
    cw = cw_ref[...]
    conv = cb_ref[...] + gbuf_ref[CONV_HALO - 2:CONV_HALO - 2 + FFN_TM, :] * cw[0:1, :]
    conv = conv + gbuf_ref[CONV_HALO - 1:CONV_HALO - 1 + FFN_TM, :] * cw[1:2, :]
    conv = conv + gate * cw[2:3, :]
    hidden = (_gelu(conv) * value).astype(BF16)
    contrib = _dot(hidden, wd_ref[...])

    @pl.when(f == 0)
    def _():
        acc_ref[...] = contrib

    @pl.when(f > 0)
    def _():
        acc_ref[...] += contrib

    @pl.when(f == n_f - 1)
    def _():
        h_new = h_ref[...] + acc_ref[...]
        if final:
            h_new = _rms(h_new, fg_ref[...])
        o_ref[...] = h_new


def _ffn(xn, h, w_gate, w_value, conv_w, conv_b, w_down, final_g, batch, seq, final):
    t = h.shape[0]
    n_s = seq // FFN_TM
    n_f = D_FF // FFN_TF
    return pl.pallas_call(
        functools.partial(_ffn_kernel, n_f=n_f, final=final),
        grid=(batch, n_s, n_f),
        in_specs=[
            pl.BlockSpec((FFN_TM, D_MODEL), lambda b, s, f: (b * n_s + s, 0)),
            pl.BlockSpec((FFN_TM, D_MODEL), lambda b, s, f: (b * n_s + s, 0)),
            pl.BlockSpec((D_MODEL, FFN_TF), lambda b, s, f: (0, f)),
            pl.BlockSpec((D_MODEL, FFN_TF), lambda b, s, f: (0, f)),
            pl.BlockSpec((3, FFN_TF), lambda b, s, f: (0, f)),
            pl.BlockSpec((1, FFN_TF), lambda b, s, f: (0, f)),
            pl.BlockSpec((FFN_TF, D_MODEL), lambda b, s, f: (f, 0)),
            pl.BlockSpec((1, D_MODEL), lambda b, s, f: (0, 0)),
        ],
        out_specs=pl.BlockSpec((FFN_TM, D_MODEL), lambda b, s, f: (b * n_s + s, 0)),
        out_shape=jax.ShapeDtypeStruct((t, D_MODEL), F32),
        scratch_shapes=[
            pltpu.VMEM((FFN_TM, D_MODEL), F32),
            pltpu.VMEM((CONV_HALO + FFN_TM, FFN_TF), F32),
            pltpu.VMEM((n_f, CONV_HALO, FFN_TF), F32),
        ],
        compiler_params=pltpu.CompilerParams(
            dimension_semantics=("arbitrary", "arbitrary", "arbitrary"),
            vmem_limit_bytes=VMEM_LIMIT),
        name="ffn",
    )(xn, h, w_gate, w_value, conv_w, conv_b, w_down, final_g)


def kernel(x, norm_mix_g, w_in, fox_bias, sgu_ln_g, sgu_w, sgu_b, w_branch,
           w_out, norm_ffn_g, w_up, conv_w, conv_b, w_down, final_g):
    batch, seq, d_model = x.shape
    assert d_model == D_MODEL and seq % FFN_TM == 0 and seq % SGU_TM == 0
    assert (batch * seq) % PROJ_TM == 0 and (batch * seq) % MERGE_TM == 0
    t = batch * seq

    f_lo, f_hi = QKV_COLS, QKV_COLS + N_HEADS
    uv_hi = f_hi + UV_COLS
    w_main = jnp.concatenate(
        [w_in[:, :, :f_lo], w_in[:, :, uv_hi:], w_in[:, :, f_hi:uv_hi]], axis=-1).astype(BF16)
    w_f = jnp.pad(w_in[:, :, f_lo:f_hi], ((0, 0), (0, 0), (0, LANES - N_HEADS))).astype(BF16)
    bias_f = jnp.pad(fox_bias, ((0, 0), (0, LANES - N_HEADS)))
    b_full = jnp.repeat(jnp.swapaxes(sgu_b, 1, 2), SGU_WIDTH // SGU_GROUPS, axis=2)
    b_full = b_full.reshape(DEPTH, SGU_CHUNK, SGU_WIDTH)
    w_br = w_branch.astype(BF16)
    w_o = w_out.astype(BF16)
    w_gate = w_up[:, :, :D_FF].astype(BF16)
    w_value = w_up[:, :, D_FF:].astype(BF16)
    w_dn = w_down.astype(BF16)

    h = x.reshape(t, D_MODEL)
    for l in range(DEPTH):
        proj, f = _proj(h, norm_mix_g[l][None, :], w_main[l], w_f[l])
        c_col = _fgate(f, bias_f[l][None, :], batch, seq)
        c_row = jnp.swapaxes(c_col.reshape(batch, seq, LANES)[:, :, :N_HEADS], 1, 2)
        c_row = c_row.reshape(batch, N_HEADS, 1, seq)
        y_c = _sgu(proj, sgu_ln_g[l][None, :], sgu_w[l], b_full[l])
        y_a = _sb_attention(proj, batch, seq)
        y_b = _fox_attention(proj, c_col, c_row, batch, seq)
        h, xn = _merge(y_a, y_b, y_c, proj, h, w_br[l], w_o[l], norm_ffn_g[l][None, :])
        h = _ffn(xn, h, w_gate[l], w_value[l], conv_w[l], conv_b[l][None, :], w_dn[l],
                 final_g[None, :], batch, seq, final=(l == DEPTH - 1))
    return h.reshape(batch, seq, D_MODEL)
```

```python
import functools

import jax
import jax.numpy as jnp
from jax import lax
from jax.experimental import pallas as pl
from jax.experimental.pallas import tpu as pltpu

F32 = jnp.float32
BF16 = jnp.bfloat16

D_MODEL = 1024
DEPTH = 4
HEAD_DIM = 64
N_HEADS = 8
ATT_WIDTH = N_HEADS * HEAD_DIM
SGU_WIDTH = 512
SGU_GROUPS = 8
SGU_CHUNK = 128
STREAM_CHUNK = 64
N_BRANCH = 3
D_FF = 2816
EPS = 1e-6

LANES = 128
HEAD_PAIRS = ATT_WIDTH // LANES
QKV_COLS = 6 * ATT_WIDTH
UV_COLS = 2 * SGU_WIDTH
GATE_COLS = N_BRANCH * D_MODEL
MAIN_COLS = QKV_COLS + UV_COLS + GATE_COLS
GATE_BLOCK = QKV_COLS // GATE_COLS
UV_BLOCK = (QKV_COLS + GATE_COLS) // UV_COLS
assert GATE_BLOCK * GATE_COLS == QKV_COLS and UV_BLOCK * UV_COLS == QKV_COLS + GATE_COLS

VMEM_LIMIT = 56 * 1024 * 1024

PROJ_TM = 1024
PROJ_TN = 1024
SGU_TM = 512
ATT_BQ = 128
ATT_BK = 128
FOX_BQ = 256
FOX_BK = 512
FOX_SUB = 128
MERGE_TM = 512
FFN_TM = 512
FFN_TF = 1408
NEG_BIG = -1e30


def _rms(x, g):
    ms = jnp.mean(x * x, axis=-1, keepdims=True)
    return x * lax.rsqrt(ms + EPS) * g


def _gelu(x):
    return 0.5 * x * (1.0 + lax.erf(x * (2.0 ** -0.5)))


def _softplus(z):
    return jnp.maximum(z, 0.0) + jnp.log(1.0 + jnp.exp(-jnp.abs(z)))


def _dot(a, b):
    return jnp.dot(a, b, preferred_element_type=F32)


def _dot_nt(a, b):
    return lax.dot_general(a, b, (((1,), (1,)), ((), ())), preferred_element_type=F32)


def _proj_kernel(x_ref, g_ref, w_ref, wf_ref, o_ref, f_ref, xn_ref):
    @pl.when(pl.program_id(1) == 0)
    def _():
        xn = _rms(x_ref[...], g_ref[...]).astype(BF16)
        xn_ref[...] = xn
        f_ref[...] = _dot(xn, wf_ref[...])

    o_ref[...] = _dot(xn_ref[...], w_ref[...]).astype(o_ref.dtype)


def _proj(h, g, w_main, w_f):
    t = h.shape[0]
    grid = (t // PROJ_TM, MAIN_COLS // PROJ_TN)
    return pl.pallas_call(
        _proj_kernel,
        grid=grid,
        in_specs=[
            pl.BlockSpec((PROJ_TM, D_MODEL), lambda i, j: (i, 0)),
            pl.BlockSpec((1, D_MODEL), lambda i, j: (0, 0)),
            pl.BlockSpec((D_MODEL, PROJ_TN), lambda i, j: (0, j)),
            pl.BlockSpec((D_MODEL, LANES), lambda i, j: (0, 0)),
        ],
        out_specs=[
            pl.BlockSpec((PROJ_TM, PROJ_TN), lambda i, j: (i, j)),
            pl.BlockSpec((PROJ_TM, LANES), lambda i, j: (i, 0)),
        ],
        out_shape=[
            jax.ShapeDtypeStruct((t, MAIN_COLS), BF16),
            jax.ShapeDtypeStruct((t, LANES), F32),
        ],
        scratch_shapes=[pltpu.VMEM((PROJ_TM, D_MODEL), BF16)],
        compiler_params=pltpu.CompilerParams(
            dimension_semantics=("arbitrary", "arbitrary"),
            vmem_limit_bytes=VMEM_LIMIT),
        name="proj",
    )(h, g, w_main, w_f)


def _fgate_kernel(f_ref, b_ref, c_ref, *, seq):
    r = lax.broadcasted_iota(jnp.int32, (LANES, LANES), 0)
    c = lax.broadcasted_iota(jnp.int32, (LANES, LANES), 1)
    lower = (c <= r).astype(F32)
    bias = b_ref[...]

    def body(i, carry):
        rows = pl.ds(pl.multiple_of(i * LANES, LANES), LANES)
        x = f_ref[rows, :] + bias
        log_f = -_softplus(-x)
        cs = jnp.dot(lower, log_f, precision=lax.Precision.HIGHEST,
                     preferred_element_type=F32) + carry
        c_ref[rows, :] = cs
        return cs[LANES - 1:LANES, :]

    lax.fori_loop(0, seq // LANES, body, jnp.zeros((1, LANES), F32))


def _fgate(f, bias, batch, seq):
    return pl.pallas_call(
        functools.partial(_fgate_kernel, seq=seq),
        grid=(batch,),
        in_specs=[
            pl.BlockSpec((seq, LANES), lambda b: (b, 0)),
            pl.BlockSpec((1, LANES), lambda b: (0, 0)),
        ],
        out_specs=pl.BlockSpec((seq, LANES), lambda b: (b, 0)),
        out_shape=jax.ShapeDtypeStruct(f.shape, F32),
        compiler_params=pltpu.CompilerParams(
            dimension_semantics=("arbitrary",), vmem_limit_bytes=VMEM_LIMIT),
        name="fgate",
    )(f, bias)


def _sgu_kernel(uv_ref, lng_ref, ws_ref, b_ref, o_ref):
    uv = _gelu(uv_ref[...].astype(F32))
    u = uv[:, :SGU_WIDTH]
    v = uv[:, SGU_WIDTH:]
    mu = jnp.mean(v, axis=-1, keepdims=True)
    vc = v - mu
    var = jnp.mean(vc * vc, axis=-1, keepdims=True)
    vn = (vc * lax.rsqrt(var + EPS) * lng_ref[...]).astype(BF16)

    t_pos = lax.broadcasted_iota(jnp.int32, (SGU_CHUNK, SGU_CHUNK), 0)
    s_pos = lax.broadcasted_iota(jnp.int32, (SGU_CHUNK, SGU_CHUNK), 1)
    mask = (s_pos // STREAM_CHUNK) <= (t_pos // STREAM_CHUNK)
    lane = lax.broadcasted_iota(jnp.int32, (SGU_CHUNK, LANES), 1)
    first_group = lane < HEAD_DIM
    group_width = SGU_WIDTH // SGU_GROUPS
    assert 2 * group_width == LANES

    for p in range(SGU_GROUPS // 2):
        cols = slice(p * LANES, (p + 1) * LANES)
        w0 = jnp.where(mask, ws_ref[2 * p], 0.0).astype(BF16)
        w1 = jnp.where(mask, ws_ref[2 * p + 1], 0.0).astype(BF16)
        bias = b_ref[:, cols]
        for c in range(SGU_TM // SGU_CHUNK):
            rows = slice(c * SGU_CHUNK, (c + 1) * SGU_CHUNK)
            vp = vn[rows, cols]
            mixed = jnp.where(first_group, _dot(w0, vp), _dot(w1, vp)) + bias
            o_ref[rows, cols] = (u[rows, cols] * mixed).astype(o_ref.dtype)


def _sgu(proj, ln_g, w_s, b_full):
    t = proj.shape[0]
    return pl.pallas_call(
        _sgu_kernel,
        grid=(t // SGU_TM,),
        in_specs=[
            pl.BlockSpec((SGU_TM, UV_COLS), lambda i: (i, UV_BLOCK)),
            pl.BlockSpec((1, SGU_WIDTH), lambda i: (0, 0)),
            pl.BlockSpec((SGU_GROUPS, SGU_CHUNK, SGU_CHUNK), lambda i: (0, 0, 0)),
            pl.BlockSpec((SGU_CHUNK, SGU_WIDTH), lambda i: (0, 0)),
        ],
        out_specs=pl.BlockSpec((SGU_TM, SGU_WIDTH), lambda i: (i, 0)),
        out_shape=jax.ShapeDtypeStruct((t, SGU_WIDTH), BF16),
        compiler_params=pltpu.CompilerParams(
            dimension_semantics=("arbitrary",), vmem_limit_bytes=VMEM_LIMIT),
        name="sgu",
    )(proj, ln_g, w_s, b_full)


SB_STAY_FLOOR = -104.0


def _sb_kernel(q_ref, k_ref, v_ref, o_ref, qm_ref, stay_ref, y_ref):
    qi = pl.program_id(1)
    lane = lax.broadcasted_iota(jnp.int32, (ATT_BQ, LANES), 1)
    first_head = lane < HEAD_DIM
    row = lax.broadcasted_iota(jnp.int32, (ATT_BQ, ATT_BK), 0)
    col = lax.broadcasted_iota(jnp.int32, (ATT_BQ, ATT_BK), 1)
    strict = col < row
    ext_row = lax.broadcasted_iota(jnp.int32, (2 * ATT_BK, ATT_BK + LANES), 0) % ATT_BK
    ext_col = lax.broadcasted_iota(jnp.int32, (2 * ATT_BK, ATT_BK + LANES), 1)
    later_ext = jnp.where((ext_col >= ATT_BK) | (ext_row > ext_col), -1.0, 0.0).astype(BF16)

    q = q_ref[...] * (HEAD_DIM ** -0.5)
    zero = jnp.zeros((ATT_BQ, LANES), q.dtype)
    for p in range(HEAD_PAIRS):
        qp = q[:, p * LANES:(p + 1) * LANES]
        qm_ref[2 * p] = jnp.where(first_head, qp, zero)
        qm_ref[2 * p + 1] = jnp.where(first_head, zero, qp)

    def step(kb, diag):
        rows = pl.ds(pl.multiple_of(kb * ATT_BK, ATT_BK), ATT_BK)
        heads = range(N_HEADS)
        cols = [slice((h // 2) * LANES, (h // 2 + 1) * LANES) for h in heads]
        z = [_dot_nt(qm_ref[h], k_ref[rows, cols[h]]) for h in heads]
        sp = [_softplus(z[h]) for h in heads]
        if diag:
            sp = [jnp.where(strict, s, 0.0) for s in sp]
        hi = [s.astype(BF16) for s in sp]
        lo = [(sp[h] - hi[h].astype(F32)).astype(BF16) for h in heads]
        ext = [_dot(jnp.concatenate([hi[h], lo[h]], axis=1), later_ext) for h in heads]
        x = [z[h] - sp[h] + ext[h][:, :ATT_BK] for h in heads]
        if diag:
            stay = [e[:, ATT_BK:] for e in ext]
            w = [jnp.where(strict, jnp.exp(xh), 0.0) for xh in x]
        else:
            stay = [stay_ref[h] for h in heads]
            w = [jnp.exp(x[h] + stay[h]) for h in heads]
            stay = [stay[h] + ext[h][:, ATT_BK:] for h in heads]
        pv = [_dot(w[h].astype(BF16), v_ref[rows, cols[h]]) for h in heads]
        top = stay[0]
        for h in heads:
            stay_ref[h] = stay[h]
            top = jnp.maximum(top, stay[h])
        for p in range(HEAD_PAIRS):
            update = jnp.where(first_head, pv[2 * p], pv[2 * p + 1])
            if diag:
                y_ref[:, cols[2 * p]] = update
            else:
                y_ref[:, cols[2 * p]] += update
        return jnp.max(top)

    def more(carry):
        kb, top = carry
        return jnp.logical_and(kb >= 0, top > SB_STAY_FLOOR)

    lax.while_loop(more, lambda c: (c[0] - 1, step(c[0], False)), (qi - 1, step(qi, True)))
    o_ref[...] = y_ref[...].astype(o_ref.dtype)


def _sb_attention(proj, batch, seq):
    nq = seq // ATT_BQ
    return pl.pallas_call(
        _sb_kernel,
        grid=(batch, nq),
        in_specs=[
            pl.BlockSpec((ATT_BQ, ATT_WIDTH), lambda b, i: (b * nq + i, 0)),
            pl.BlockSpec((seq, ATT_WIDTH), lambda b, i: (b, 1)),
            pl.BlockSpec((seq, ATT_WIDTH), lambda b, i: (b, 2)),
        ],
        out_specs=pl.BlockSpec((ATT_BQ, ATT_WIDTH), lambda b, i: (b * nq + i, 0)),
        out_shape=jax.ShapeDtypeStruct((batch * seq, ATT_WIDTH), BF16),
        scratch_shapes=[
            pltpu.VMEM((N_HEADS, ATT_BQ, LANES), BF16),
            pltpu.VMEM((N_HEADS, ATT_BQ, LANES), F32),
            pltpu.VMEM((ATT_BQ, ATT_WIDTH), F32),
        ],
        compiler_params=pltpu.CompilerParams(
            dimension_semantics=("arbitrary", "arbitrary"),
            vmem_limit_bytes=VMEM_LIMIT),
        name="sb_attention",
    )(proj, proj, proj)


def _fox_kernel(q_ref, k_ref, v_ref, ccol_ref, crow0_ref, crow1_ref, o_ref,
                m_ref, acc_ref):
    hp = pl.program_id(1)
    qi = pl.program_id(2)
    blocks_per_kb = FOX_BK // FOX_BQ
    kd = qi // blocks_per_kb
    q_off = (qi % blocks_per_kb) * FOX_BQ
    lane = lax.broadcasted_iota(jnp.int32, (FOX_BQ, LANES), 1)
    first_head = lane < HEAD_DIM
    q = q_ref[...] * (HEAD_DIM ** -0.5)
    zero = jnp.zeros_like(q)
    qm = (jnp.where(first_head, q, zero), jnp.where(first_head, zero, q))
    ccol = ccol_ref[...]
    c_t = tuple(jnp.sum(jnp.where(lane == 2 * hp + hh, ccol, 0.0), axis=1, keepdims=True)
                for hh in range(2))
    crow = (crow0_ref, crow1_ref)

    chains = [(hh, slice(r * FOX_SUB, (r + 1) * FOX_SUB))
              for hh in range(2) for r in range(FOX_BQ // FOX_SUB)]
    ones = jnp.ones((FOX_BK, LANES), BF16)

    def block(kb, diag):
        rows = pl.ds(pl.multiple_of(kb * FOX_BK, FOX_BK), FOX_BK)
        k = k_ref[rows, :]
        v_ext = jnp.concatenate([v_ref[rows, :], ones], axis=1)
        c_s = [crow[hh][0, 0, :, rows] for hh in range(2)]
        z = [_dot_nt(qm[hh][rs, :], k) + (c_t[hh][rs, :] - c_s[hh]) for hh, rs in chains]
        if diag:
            row = lax.broadcasted_iota(jnp.int32, (FOX_SUB, FOX_BK), 0)
            col = lax.broadcasted_iota(jnp.int32, (FOX_SUB, FOX_BK), 1)
            z = [jnp.where(col <= row + (q_off + rs.start), zc, NEG_BIG)
                 for zc, (hh, rs) in zip(z, chains)]
            m_new = [jnp.max(zc, axis=1, keepdims=True) for zc in z]
        else:
            m_old = [m_ref[hh, rs, :] for hh, rs in chains]
            m_new = [jnp.maximum(mo, jnp.max(zc, axis=1, keepdims=True))
                     for mo, zc in zip(m_old, z)]
            alpha = [jnp.exp(mo - mn) for mo, mn in zip(m_old, m_new)]
        p = [jnp.exp(zc - mn).astype(BF16) for zc, mn in zip(z, m_new)]
        pv = [_dot(pc, v_ext) for pc in p]
        for i, (hh, rs) in enumerate(chains):
            m_ref[hh, rs, :] = m_new[i]
            if diag:
                acc_ref[hh, rs, :] = pv[i]
            else:
                acc_ref[hh, rs, :] = alpha[i] * acc_ref[hh, rs, :] + pv[i]

    block(kd, True)

    def body(i, carry):
        block(kd - 1 - i, False)
        return carry

    lax.fori_loop(0, kd, body, 0)
    y = [acc_ref[hh, :, :LANES] / acc_ref[hh, :, LANES:] for hh in range(2)]
    o_ref[...] = jnp.where(first_head, y[0], y[1]).astype(o_ref.dtype)


def _fox_attention(proj, c_col, c_row, batch, seq):
    nq = seq // FOX_BQ
    return pl.pallas_call(
        _fox_kernel,
        grid=(batch, HEAD_PAIRS, nq),
        in_specs=[
            pl.BlockSpec((FOX_BQ, LANES), lambda b, p, i: (b * nq + i, 3 * HEAD_PAIRS + p)),
            pl.BlockSpec((seq, LANES), lambda b, p, i: (b, 4 * HEAD_PAIRS + p)),
            pl.BlockSpec((seq, LANES), lambda b, p, i: (b, 5 * HEAD_PAIRS + p)),
            pl.BlockSpec((FOX_BQ, LANES), lambda b, p, i: (b * nq + i, 0)),
            pl.BlockSpec((1, 1, 1, seq), lambda b, p, i: (b, 2 * p, 0, 0)),
            pl.BlockSpec((1, 1, 1, seq), lambda b, p, i: (b, 2 * p + 1, 0, 0)),
        ],
        out_specs=pl.BlockSpec((FOX_BQ, LANES), lambda b, p, i: (b * nq + i, p)),
        out_shape=jax.ShapeDtypeStruct((batch * seq, ATT_WIDTH), BF16),
        scratch_shapes=[
            pltpu.VMEM((2, FOX_BQ, 1), F32),
            pltpu.VMEM((2, FOX_BQ, 2 * LANES), F32),
        ],
        compiler_params=pltpu.CompilerParams(
            dimension_semantics=("arbitrary", "arbitrary", "arbitrary"),
            vmem_limit_bytes=VMEM_LIMIT),
        name="fox_attention",
    )(proj, proj, proj, c_col, c_row, c_row)


def _merge_kernel(ya_ref, yb_ref, yc_ref, gate_ref, h_ref, wbr_ref, wout_ref, g_ref,
                  h_out_ref, xn_out_ref):
    merged = None
    for i, y_ref in enumerate((ya_ref, yb_ref, yc_ref)):
        gate = gate_ref[:, i * D_MODEL:(i + 1) * D_MODEL].astype(F32)
        term = (1.0 / (1.0 + jnp.exp(-gate))) * _dot(y_ref[...], wbr_ref[i])
        merged = term if merged is None else merged + term
    h_new = h_ref[...] + _dot(merged.astype(BF16), wout_ref[...])
    h_out_ref[...] = h_new
    xn_out_ref[...] = _rms(h_new, g_ref[...]).astype(xn_out_ref.dtype)


def _merge(y_a, y_b, y_c, proj, h, w_br, w_out, g_ffn):
    t = h.shape[0]
    y_spec = pl.BlockSpec((MERGE_TM, ATT_WIDTH), lambda i: (i, 0))
    return pl.pallas_call(
        _merge_kernel,
        grid=(t // MERGE_TM,),
        in_specs=[
            y_spec, y_spec, y_spec,
            pl.BlockSpec((MERGE_TM, GATE_COLS), lambda i: (i, GATE_BLOCK)),
            pl.BlockSpec((MERGE_TM, D_MODEL), lambda i: (i, 0)),
            pl.BlockSpec((N_BRANCH, ATT_WIDTH, D_MODEL), lambda i: (0, 0, 0)),
            pl.BlockSpec((D_MODEL, D_MODEL), lambda i: (0, 0)),
            pl.BlockSpec((1, D_MODEL), lambda i: (0, 0)),
        ],
        out_specs=[
            pl.BlockSpec((MERGE_TM, D_MODEL), lambda i: (i, 0)),
            pl.BlockSpec((MERGE_TM, D_MODEL), lambda i: (i, 0)),
        ],
        out_shape=[
            jax.ShapeDtypeStruct((t, D_MODEL), F32),
            jax.ShapeDtypeStruct((t, D_MODEL), BF16),
        ],
        compiler_params=pltpu.CompilerParams(
            dimension_semantics=("arbitrary",), vmem_limit_bytes=VMEM_LIMIT),
        name="merge",
    )(y_a, y_b, y_c, proj, h, w_br, w_out, g_ffn)


CONV_HALO = 8


def _ffn_kernel(xn_ref, h_ref, wg_ref, wv_ref, cw_ref, cb_ref, wd_ref, fg_ref, o_ref,
                acc_ref, gbuf_ref, carry_ref, *, n_f, final):
    si = pl.program_id(1)
    f = pl.program_id(2)
    x = xn_ref[...]
    gate = _dot(x, wg_ref[...])
    value = _dot(x, wv_ref[...])

    @pl.when(si == 0)
    def _():
        carry_ref[f] = jnp.zeros((CONV_HALO, FFN_TF), F32)

    gbuf_ref[0:CONV_HALO, :] = carry_ref[f]
    gbuf_ref[CONV_HALO:CONV_HALO + FFN_TM, :] = gate
    carry_ref[f] = gate[FFN_TM - CONV_HALO:FFN_TM, :]
    cw = cw_ref[...]
    conv = cb_ref[...] + gbuf_ref[CONV_HALO - 2:CONV_HALO - 2 + FFN_TM, :] * cw[0:1, :]
    conv = conv + gbuf_ref[CONV_HALO - 1:CONV_HALO - 1 + FFN_TM, :] * cw[1:2, :]
    conv = conv + gate * cw[2:3, :]
    hidden = (_gelu(conv) * value).astype(BF16)
    contrib = _dot(hidden, wd_ref[...])

    @pl.when(f == 0)
    def _():
        acc_ref[...] = contrib

    @pl.when(f > 0)
    def _():
        acc_ref[...] += contrib

    @pl.when(f == n_f - 1)
    def _():
        h_new = h_ref[...] + acc_ref[...]
        if final:
            h_new = _rms(h_new, fg_ref[...])
        o_ref[...] = h_new


def _ffn(xn, h, w_gate, w_value, conv_w, conv_b, w_down, final_g, batch, seq, final):
    t = h.shape[0]
    n_s = seq // FFN_TM
    n_f = D_FF // FFN_TF
    return pl.pallas_call(
        functools.partial(_ffn_kernel, n_f=n_f, final=final),
        grid=(batch, n_s, n_f),
        in_specs=[
            pl.BlockSpec((FFN_TM, D_MODEL), lambda b, s, f: (b * n_s + s, 0)),
            pl.BlockSpec((FFN_TM, D_MODEL), lambda b, s, f: (b * n_s + s, 0)),
            pl.BlockSpec((D_MODEL, FFN_TF), lambda b, s, f: (0, f)),
            pl.BlockSpec((D_MODEL, FFN_TF), lambda b, s, f: (0, f)),
            pl.BlockSpec((3, FFN_TF), lambda b, s, f: (0, f)),
            pl.BlockSpec((1, FFN_TF), lambda b, s, f: (0, f)),
            pl.BlockSpec((FFN_TF, D_MODEL), lambda b, s, f: (f, 0)),
            pl.BlockSpec((1, D_MODEL), lambda b, s, f: (0, 0)),
        ],
        out_specs=pl.BlockSpec((FFN_TM, D_MODEL), lambda b, s, f: (b * n_s + s, 0)),
        out_shape=jax.ShapeDtypeStruct((t, D_MODEL), F32),
        scratch_shapes=[
            pltpu.VMEM((FFN_TM, D_MODEL), F32),
            pltpu.VMEM((CONV_HALO + FFN_TM, FFN_TF), F32),
            pltpu.VMEM((n_f, CONV_HALO, FFN_TF), F32),
        ],
        compiler_params=pltpu.CompilerParams(
            dimension_semantics=("arbitrary", "arbitrary", "arbitrary"),
            vmem_limit_bytes=VMEM_LIMIT),
        name="ffn",
    )(xn, h, w_gate, w_value, conv_w, conv_b, w_down, final_g)


def kernel(x, norm_mix_g, w_in, fox_bias, sgu_ln_g, sgu_w, sgu_b, w_branch,
           w_out, norm_ffn_g, w_up, conv_w, conv_b, w_down, final_g):
    batch, seq, d_model = x.shape
    assert d_model == D_MODEL and seq % FFN_TM == 0 and seq % SGU_TM == 0
    assert seq % FOX_BK == 0 and FOX_BK % FOX_BQ == 0 and seq % ATT_BQ == 0
    assert (batch * seq) % PROJ_TM == 0 and (batch * seq) % MERGE_TM == 0
    t = batch * seq

    f_lo, f_hi = QKV_COLS, QKV_COLS + N_HEADS
    uv_hi = f_hi + UV_COLS
    w_main = jnp.concatenate(
        [w_in[:, :, :f_lo], w_in[:, :, uv_hi:], w_in[:, :, f_hi:uv_hi]], axis=-1).astype(BF16)
    w_f = jnp.pad(w_in[:, :, f_lo:f_hi], ((0, 0), (0, 0), (0, LANES - N_HEADS))).astype(BF16)
    bias_f = jnp.pad(fox_bias, ((0, 0), (0, LANES - N_HEADS)))
    b_full = jnp.repeat(jnp.swapaxes(sgu_b, 1, 2), SGU_WIDTH // SGU_GROUPS, axis=2)
    b_full = b_full.reshape(DEPTH, SGU_CHUNK, SGU_WIDTH)
    w_br = w_branch.astype(BF16)
    w_o = w_out.astype(BF16)
    w_gate = w_up[:, :, :D_FF].astype(BF16)
    w_value = w_up[:, :, D_FF:].astype(BF16)
    w_dn = w_down.astype(BF16)

    h = x.reshape(t, D_MODEL)
    for l in range(DEPTH):
        proj, f = _proj(h, norm_mix_g[l][None, :], w_main[l], w_f[l])
        c_col = _fgate(f, bias_f[l][None, :], batch, seq)
        c_row = jnp.swapaxes(c_col.reshape(batch, seq, LANES)[:, :, :N_HEADS], 1, 2)
        c_row = c_row.reshape(batch, N_HEADS, 1, seq)
        y_c = _sgu(proj, sgu_ln_g[l][None, :], sgu_w[l], b_full[l])
        y_a = _sb_attention(proj, batch, seq)
        y_b = _fox_attention(proj, c_col, c_row, batch, seq)
        h, xn = _merge(y_a, y_b, y_c, proj, h, w_br[l], w_o[l], norm_ffn_g[l][None, :])
        h = _ffn(xn, h, w_gate[l], w_value[l], conv_w[l], conv_b[l][None, :], w_dn[l],
                 final_g[None, :], batch, seq, final=(l == DEPTH - 1))
    return h.reshape(batch, seq, D_MODEL)
```

```python
import functools

import jax
import jax.numpy as jnp
from jax import lax
from jax.experimental import pallas as pl
from jax.experimental.pallas import tpu as pltpu

F32 = jnp.float32
BF16 = jnp.bfloat16

D_MODEL = 1024
DEPTH = 4
HEAD_DIM = 64
N_HEADS = 8
ATT_WIDTH = N_HEADS * HEAD_DIM
SGU_WIDTH = 512
SGU_GROUPS = 8
SGU_CHUNK = 128
STREAM_CHUNK = 64
N_BRANCH = 3
D_FF = 2816
EPS = 1e-6

LANES = 128
HEAD_PAIRS = ATT_WIDTH // LANES
QKV_COLS = 6 * ATT_WIDTH
UV_COLS = 2 * SGU_WIDTH
GATE_COLS = N_BRANCH * D_MODEL
MAIN_COLS = QKV_COLS + UV_COLS + GATE_COLS
GATE_BLOCK = QKV_COLS // GATE_COLS
UV_BLOCK = (QKV_COLS + GATE_COLS) // UV_COLS
assert GATE_BLOCK * GATE_COLS == QKV_COLS and UV_BLOCK * UV_COLS == QKV_COLS + GATE_COLS

VMEM_LIMIT = 56 * 1024 * 1024

PROJ_TM = 1024
PROJ_TN = 1024
SGU_TM = 512
ATT_BQ = 128
ATT_BK = 128
FOX_BQ = 256
FOX_BK = 512
FOX_SUB = 128
MERGE_TM = 512
FFN_TM = 512
FFN_TF = 1408
NEG_BIG = -1e30


def _rms(x, g):
    ms = jnp.mean(x * x, axis=-1, keepdims=True)
    return x * lax.rsqrt(ms + EPS) * g


def _gelu(x):
    return 0.5 * x * (1.0 + lax.erf(x * (2.0 ** -0.5)))


def _softplus(z):
    return jnp.maximum(z, 0.0) + jnp.log(1.0 + jnp.exp(-jnp.abs(z)))


def _dot(a, b):
    return jnp.dot(a, b, preferred_element_type=F32)


def _dot_nt(a, b):
    return lax.dot_general(a, b, (((1,), (1,)), ((), ())), preferred_element_type=F32)


def _proj_kernel(x_ref, g_ref, w_ref, wf_ref, o_ref, f_ref, xn_ref):
    @pl.when(pl.program_id(1) == 0)
    def _():
        xn = _rms(x_ref[...], g_ref[...]).astype(BF16)
        xn_ref[...] = xn
        f_ref[...] = _dot(xn, wf_ref[...])

    o_ref[...] = _dot(xn_ref[...], w_ref[...]).astype(o_ref.dtype)


def _proj(h, g, w_main, w_f):
    t = h.shape[0]
    grid = (t // PROJ_TM, MAIN_COLS // PROJ_TN)
    return pl.pallas_call(
        _proj_kernel,
        grid=grid,
        in_specs=[
            pl.BlockSpec((PROJ_TM, D_MODEL), lambda i, j: (i, 0)),
            pl.BlockSpec((1, D_MODEL), lambda i, j: (0, 0)),
            pl.BlockSpec((D_MODEL, PROJ_TN), lambda i, j: (0, j)),
            pl.BlockSpec((D_MODEL, LANES), lambda i, j: (0, 0)),
        ],
        out_specs=[
            pl.BlockSpec((PROJ_TM, PROJ_TN), lambda i, j: (i, j)),
            pl.BlockSpec((PROJ_TM, LANES), lambda i, j: (i, 0)),
        ],
        out_shape=[
            jax.ShapeDtypeStruct((t, MAIN_COLS), BF16),
            jax.ShapeDtypeStruct((t, LANES), F32),
        ],
        scratch_shapes=[pltpu.VMEM((PROJ_TM, D_MODEL), BF16)],
        compiler_params=pltpu.CompilerParams(
            dimension_semantics=("arbitrary", "arbitrary"),
            vmem_limit_bytes=VMEM_LIMIT),
        name="proj",
    )(h, g, w_main, w_f)


def _fgate_kernel(f_ref, b_ref, c_ref, *, seq):
    r = lax.broadcasted_iota(jnp.int32, (LANES, LANES), 0)
    c = lax.broadcasted_iota(jnp.int32, (LANES, LANES), 1)
    lower = (c <= r).astype(F32)
    bias = b_ref[...]

    def body(i, carry):
        rows = pl.ds(pl.multiple_of(i * LANES, LANES), LANES)
        x = f_ref[rows, :] + bias
        log_f = -_softplus(-x)
        cs = jnp.dot(lower, log_f, precision=lax.Precision.HIGHEST,
                     preferred_element_type=F32) + carry
        c_ref[rows, :] = cs
        return cs[LANES - 1:LANES, :]

    lax.fori_loop(0, seq // LANES, body, jnp.zeros((1, LANES), F32))


def _fgate(f, bias, batch, seq):
    return pl.pallas_call(
        functools.partial(_fgate_kernel, seq=seq),
        grid=(batch,),
        in_specs=[
            pl.BlockSpec((seq, LANES), lambda b: (b, 0)),
            pl.BlockSpec((1, LANES), lambda b: (0, 0)),
        ],
        out_specs=pl.BlockSpec((seq, LANES), lambda b: (b, 0)),
        out_shape=jax.ShapeDtypeStruct(f.shape, F32),
        compiler_params=pltpu.CompilerParams(
            dimension_semantics=("arbitrary",), vmem_limit_bytes=VMEM_LIMIT),
        name="fgate",
    )(f, bias)


def _sgu_kernel(uv_ref, lng_ref, ws_ref, b_ref, o_ref):
    uv = _gelu(uv_ref[...].astype(F32))
    u = uv[:, :SGU_WIDTH]
    v = uv[:, SGU_WIDTH:]
    mu = jnp.mean(v, axis=-1, keepdims=True)
    vc = v - mu
    var = jnp.mean(vc * vc, axis=-1, keepdims=True)
    vn = (vc * lax.rsqrt(var + EPS) * lng_ref[...]).astype(BF16)

    t_pos = lax.broadcasted_iota(jnp.int32, (SGU_CHUNK, SGU_CHUNK), 0)
    s_pos = lax.broadcasted_iota(jnp.int32, (SGU_CHUNK, SGU_CHUNK), 1)
    mask = (s_pos // STREAM_CHUNK) <= (t_pos // STREAM_CHUNK)
    lane = lax.broadcasted_iota(jnp.int32, (SGU_CHUNK, LANES), 1)
    first_group = lane < HEAD_DIM
    group_width = SGU_WIDTH // SGU_GROUPS
    assert 2 * group_width == LANES

    for p in range(SGU_GROUPS // 2):
        cols = slice(p * LANES, (p + 1) * LANES)
        w0 = jnp.where(mask, ws_ref[2 * p], 0.0).astype(BF16)
        w1 = jnp.where(mask, ws_ref[2 * p + 1], 0.0).astype(BF16)
        bias = b_ref[:, cols]
        for c in range(SGU_TM // SGU_CHUNK):
            rows = slice(c * SGU_CHUNK, (c + 1) * SGU_CHUNK)
            vp = vn[rows, cols]
            mixed = jnp.where(first_group, _dot(w0, vp), _dot(w1, vp)) + bias
            o_ref[rows, cols] = (u[rows, cols] * mixed).astype(o_ref.dtype)


def _sgu(proj, ln_g, w_s, b_full):
    t = proj.shape[0]
    return pl.pallas_call(
        _sgu_kernel,
        grid=(t // SGU_TM,),
        in_specs=[
            pl.BlockSpec((SGU_TM, UV_COLS), lambda i: (i, UV_BLOCK)),
            pl.BlockSpec((1, SGU_WIDTH), lambda i: (0, 0)),
            pl.BlockSpec((SGU_GROUPS, SGU_CHUNK, SGU_CHUNK), lambda i: (0, 0, 0)),
            pl.BlockSpec((SGU_CHUNK, SGU_WIDTH), lambda i: (0, 0)),
        ],
        out_specs=pl.BlockSpec((SGU_TM, SGU_WIDTH), lambda i: (i, 0)),
        out_shape=jax.ShapeDtypeStruct((t, SGU_WIDTH), BF16),
        compiler_params=pltpu.CompilerParams(
            dimension_semantics=("arbitrary",), vmem_limit_bytes=VMEM_LIMIT),
        name="sgu",
    )(proj, ln_g, w_s, b_full)


SB_STAY_FLOOR = -104.0


def _sb_kernel(q_ref, k_ref, v_ref, o_ref, qm_ref, stay_ref, y_ref):
    qi = pl.program_id(1)
    lane = lax.broadcasted_iota(jnp.int32, (ATT_BQ, LANES), 1)
    first_head = lane < HEAD_DIM
    row = lax.broadcasted_iota(jnp.int32, (ATT_BQ, ATT_BK), 0)
    col = lax.broadcasted_iota(jnp.int32, (ATT_BQ, ATT_BK), 1)
    strict = col < row
    ext_row = lax.broadcasted_iota(jnp.int32, (2 * ATT_BK, ATT_BK + LANES), 0) % ATT_BK
    ext_col = lax.broadcasted_iota(jnp.int32, (2 * ATT_BK, ATT_BK + LANES), 1)
    later_ext = jnp.where((ext_col >= ATT_BK) | (ext_row > ext_col), -1.0, 0.0).astype(BF16)

    q = q_ref[...] * (HEAD_DIM ** -0.5)
    zero = jnp.zeros((ATT_BQ, LANES), q.dtype)
    for p in range(HEAD_PAIRS):
        qp = q[:, p * LANES:(p + 1) * LANES]
        qm_ref[2 * p] = jnp.where(first_head, qp, zero)
        qm_ref[2 * p + 1] = jnp.where(first_head, zero, qp)

    def step(kb, diag):
        rows = pl.ds(pl.multiple_of(kb * ATT_BK, ATT_BK), ATT_BK)
        heads = range(N_HEADS)
        cols = [slice((h // 2) * LANES, (h // 2 + 1) * LANES) for h in heads]
        z = [_dot_nt(qm_ref[h], k_ref[rows, cols[h]]) for h in heads]
        sp = [_softplus(z[h]) for h in heads]
        if diag:
            sp = [jnp.where(strict, s, 0.0) for s in sp]
        hi = [s.astype(BF16) for s in sp]
        lo = [(sp[h] - hi[h].astype(F32)).astype(BF16) for h in heads]
        ext = [_dot(jnp.concatenate([hi[h], lo[h]], axis=1), later_ext) for h in heads]
        x = [z[h] - sp[h] + ext[h][:, :ATT_BK] for h in heads]
        if diag:
            stay = [e[:, ATT_BK:] for e in ext]
            w = [jnp.where(strict, jnp.exp(xh), 0.0) for xh in x]
        else:
            stay = [stay_ref[h] for h in heads]
            w = [jnp.exp(x[h] + stay[h]) for h in heads]
            stay = [stay[h] + ext[h][:, ATT_BK:] for h in heads]
        pv = [_dot(w[h].astype(BF16), v_ref[rows, cols[h]]) for h in heads]
        top = stay[0]
        for h in heads:
            stay_ref[h] = stay[h]
            top = jnp.maximum(top, stay[h])
        for p in range(HEAD_PAIRS):
            update = jnp.where(first_head, pv[2 * p], pv[2 * p + 1])
            if diag:
                y_ref[:, cols[2 * p]] = update
            else:
                y_ref[:, cols[2 * p]] += update
        return jnp.max(top)

    def more(carry):
        kb, top = carry
        return jnp.logical_and(kb >= 0, top > SB_STAY_FLOOR)

    lax.while_loop(more, lambda c: (c[0] - 1, step(c[0], False)), (qi - 1, step(qi, True)))
    o_ref[...] = y_ref[...].astype(o_ref.dtype)


def _sb_attention(proj, batch, seq):
    nq = seq // ATT_BQ
    return pl.pallas_call(
        _sb_kernel,
        grid=(batch, nq),
        in_specs=[
            pl.BlockSpec((ATT_BQ, ATT_WIDTH), lambda b, i: (b * nq + i, 0)),
            pl.BlockSpec((seq, ATT_WIDTH), lambda b, i: (b, 1)),
            pl.BlockSpec((seq, ATT_WIDTH), lambda b, i: (b, 2)),
        ],
        out_specs=pl.BlockSpec((ATT_BQ, ATT_WIDTH), lambda b, i: (b * nq + i, 0)),
        out_shape=jax.ShapeDtypeStruct((batch * seq, ATT_WIDTH), BF16),
        scratch_shapes=[
            pltpu.VMEM((N_HEADS, ATT_BQ, LANES), BF16),
            pltpu.VMEM((N_HEADS, ATT_BQ, LANES), F32),
            pltpu.VMEM((ATT_BQ, ATT_WIDTH), F32),
        ],
        compiler_params=pltpu.CompilerParams(
            dimension_semantics=("arbitrary", "arbitrary"),
            vmem_limit_bytes=VMEM_LIMIT),
        name="sb_attention",
    )(proj, proj, proj)


FOX_ZERO_LOGIT = -106.0
NORM_SLACK = 1.001


def _fox_kernel(cend_ref, q_ref, k_ref, v_ref, ccol_ref, crow0_ref, crow1_ref, o_ref,
                qm_ref, ct_ref, reach_ref, m_ref, acc_ref, p_ref, alpha_ref, knorm_ref, *, n_kb):
    b = pl.program_id(0)
    hp = pl.program_id(1)
    qi = pl.program_id(2)
    blocks_per_kb = FOX_BK // FOX_BQ
    kd = qi // blocks_per_kb
    q_off = (qi % blocks_per_kb) * FOX_BQ
    lane = lax.broadcasted_iota(jnp.int32, (FOX_BQ, LANES), 1)
    first_head = lane < HEAD_DIM
    head_lanes = (first_head, jnp.logical_not(first_head))

    @pl.when(qi == 0)
    def _():
        kk = k_ref[...].astype(F32)
        kk = kk * kk
        klane = lax.broadcasted_iota(jnp.int32, kk.shape, 1) < HEAD_DIM
        knorm_ref[0] = jnp.max(jnp.sqrt(jnp.sum(jnp.where(klane, kk, 0.0), axis=1, keepdims=True)))
        knorm_ref[1] = jnp.max(jnp.sqrt(jnp.sum(jnp.where(klane, 0.0, kk), axis=1, keepdims=True)))

    q = q_ref[...] * (HEAD_DIM ** -0.5)
    zero = jnp.zeros_like(q)
    q32 = q.astype(F32)
    q32 = q32 * q32
    ccol = ccol_ref[...]
    crow = (crow0_ref, crow1_ref)
    for hh in range(2):
        qm_ref[hh] = jnp.where(head_lanes[hh], q, zero)
        c_t = jnp.sum(jnp.where(lane == 2 * hp + hh, ccol, 0.0), axis=1, keepdims=True)
        qnorm = jnp.sqrt(jnp.sum(jnp.where(head_lanes[hh], q32, 0.0), axis=1, keepdims=True))
        ct_ref[hh] = jnp.broadcast_to(c_t, (FOX_BQ, LANES))
        reach_ref[hh] = jnp.broadcast_to(qnorm * knorm_ref[hh] * NORM_SLACK + c_t, (FOX_BQ, LANES))
    lane_tiles = FOX_BK // LANES

    chains = [(hh, slice(r * FOX_SUB, (r + 1) * FOX_SUB))
              for hh in range(2) for r in range(FOX_BQ // FOX_SUB)]
    ones = jnp.ones((FOX_BK, LANES), BF16)

    row = lax.broadcasted_iota(jnp.int32, (FOX_SUB, FOX_BK), 0)
    col = lax.broadcasted_iota(jnp.int32, (FOX_SUB, FOX_BK), 1)

    def flush(kb_prev):
        rows = pl.ds(pl.multiple_of(kb_prev * FOX_BK, FOX_BK), FOX_BK)
        v_ext = jnp.concatenate([v_ref[rows, :], ones], axis=1)
        pv = [_dot(p_ref[ci], v_ext) for ci in range(len(chains))]
        for ci, (hh, rs) in enumerate(chains):
            alpha = alpha_ref[ci]
            acc_ref[hh, rs, :] = jnp.concatenate([alpha, alpha], axis=1) * acc_ref[hh, rs, :] + pv[ci]

    def block(kb, diag):
        rows = pl.ds(pl.multiple_of(kb * FOX_BK, FOX_BK), FOX_BK)
        k = k_ref[rows, :]
        c_s = [crow[hh][0, 0, :, rows] for hh in range(2)]
        z = [_dot_nt(qm_ref[hh, rs, :], k)
             + (jnp.concatenate([ct_ref[hh, rs, :]] * lane_tiles, axis=1) - c_s[hh])
             for hh, rs in chains]
        if diag:
            z = [jnp.where(col <= row + (q_off + rs.start), zc, NEG_BIG)
                 for zc, (hh, rs) in zip(z, chains)]
        else:
            flush(kb + 1)
        worst = [None, None]
        for ci, (hh, rs) in enumerate(chains):
            zmax = jnp.broadcast_to(jnp.max(z[ci], axis=1, keepdims=True), (FOX_SUB, LANES))
            if diag:
                m_new = zmax
                alpha_ref[ci] = jnp.zeros_like(zmax)
            else:
                m_old = m_ref[hh, rs, :]
                m_new = jnp.maximum(m_old, zmax)
                alpha_ref[ci] = jnp.exp(m_old - m_new)
            p_ref[ci] = jnp.exp(z[ci] - jnp.concatenate([m_new] * lane_tiles, axis=1)).astype(BF16)
            m_ref[hh, rs, :] = m_new
            gap = reach_ref[hh, rs, :] - m_new
            worst[hh] = gap if worst[hh] is None else jnp.maximum(worst[hh], gap)
        live = None
        kb_next = jnp.maximum(kb - 1, 0)
        for hh in range(2):
            c_end = cend_ref[(b * N_HEADS + 2 * hp + hh) * n_kb + kb_next]
            alive = jnp.max(worst[hh]) - c_end > FOX_ZERO_LOGIT
            live = alive if live is None else jnp.logical_or(live, alive)
        return live

    acc_ref[...] = jnp.zeros(acc_ref.shape, F32)
    kb_last, _ = lax.while_loop(lambda c: jnp.logical_and(c[0] >= 0, c[1]),
                                lambda c: (c[0] - 1, block(c[0], False)),
                                (kd - 1, block(kd, True)))
    flush(kb_last + 1)
    y = [acc_ref[hh, :, :LANES] / acc_ref[hh, :, LANES:] for hh in range(2)]
    o_ref[...] = jnp.where(first_head, y[0], y[1]).astype(o_ref.dtype)


def _fox_attention(proj, c_col, c_row, batch, seq):
    nq = seq // FOX_BQ
    n_kb = seq // FOX_BK
    c_end = c_row[:, :, 0, FOX_BK - 1::FOX_BK].reshape(batch * N_HEADS * n_kb)
    return pl.pallas_call(
        functools.partial(_fox_kernel, n_kb=n_kb),
        grid=(batch, HEAD_PAIRS, nq),
        in_specs=[
            pl.BlockSpec(memory_space=pltpu.SMEM),
            pl.BlockSpec((FOX_BQ, LANES), lambda b, p, i: (b * nq + i, 3 * HEAD_PAIRS + p)),
            pl.BlockSpec((seq, LANES), lambda b, p, i: (b, 4 * HEAD_PAIRS + p)),
            pl.BlockSpec((seq, LANES), lambda b, p, i: (b, 5 * HEAD_PAIRS + p)),
            pl.BlockSpec((FOX_BQ, LANES), lambda b, p, i: (b * nq + i, 0)),
            pl.BlockSpec((1, 1, 1, seq), lambda b, p, i: (b, 2 * p, 0, 0)),
            pl.BlockSpec((1, 1, 1, seq), lambda b, p, i: (b, 2 * p + 1, 0, 0)),
        ],
        out_specs=pl.BlockSpec((FOX_BQ, LANES), lambda b, p, i: (b * nq + i, p)),
        out_shape=jax.ShapeDtypeStruct((batch * seq, ATT_WIDTH), BF16),
        scratch_shapes=[
            pltpu.VMEM((2, FOX_BQ, LANES), BF16),
            pltpu.VMEM((2, FOX_BQ, LANES), F32),
            pltpu.VMEM((2, FOX_BQ, LANES), F32),
            pltpu.VMEM((2, FOX_BQ, LANES), F32),
            pltpu.VMEM((2, FOX_BQ, 2 * LANES), F32),
            pltpu.VMEM((2 * FOX_BQ // FOX_SUB, FOX_SUB, FOX_BK), BF16),
            pltpu.VMEM((2 * FOX_BQ // FOX_SUB, FOX_SUB, LANES), F32),
            pltpu.SMEM((2,), F32),
        ],
        compiler_params=pltpu.CompilerParams(
            dimension_semantics=("arbitrary", "arbitrary", "arbitrary"),
            vmem_limit_bytes=VMEM_LIMIT),
        name="fox_attention",
    )(c_end, proj, proj, proj, c_col, c_row, c_row)


def _merge_kernel(ya_ref, yb_ref, yc_ref, gate_ref, h_ref, wbr_ref, wout_ref, g_ref,
                  h_out_ref, xn_out_ref):
    merged = None
    for i, y_ref in enumerate((ya_ref, yb_ref, yc_ref)):
        gate = gate_ref[:, i * D_MODEL:(i + 1) * D_MODEL].astype(F32)
        term = (1.0 / (1.0 + jnp.exp(-gate))) * _dot(y_ref[...], wbr_ref[i])
        merged = term if merged is None else merged + term
    h_new = h_ref[...] + _dot(merged.astype(BF16), wout_ref[...])
    h_out_ref[...] = h_new
    xn_out_ref[...] = _rms(h_new, g_ref[...]).astype(xn_out_ref.dtype)


def _merge(y_a, y_b, y_c, proj, h, w_br, w_out, g_ffn):
    t = h.shape[0]
    y_spec = pl.BlockSpec((MERGE_TM, ATT_WIDTH), lambda i: (i, 0))
    return pl.pallas_call(
        _merge_kernel,
        grid=(t // MERGE_TM,),
        in_specs=[
            y_spec, y_spec, y_spec,
            pl.BlockSpec((MERGE_TM, GATE_COLS), lambda i: (i, GATE_BLOCK)),
            pl.BlockSpec((MERGE_TM, D_MODEL), lambda i: (i, 0)),
            pl.BlockSpec((N_BRANCH, ATT_WIDTH, D_MODEL), lambda i: (0, 0, 0)),
            pl.BlockSpec((D_MODEL, D_MODEL), lambda i: (0, 0)),
            pl.BlockSpec((1, D_MODEL), lambda i: (0, 0)),
        ],
        out_specs=[
            pl.BlockSpec((MERGE_TM, D_MODEL), lambda i: (i, 0)),
            pl.BlockSpec((MERGE_TM, D_MODEL), lambda i: (i, 0)),
        ],
        out_shape=[
            jax.ShapeDtypeStruct((t, D_MODEL), F32),
            jax.ShapeDtypeStruct((t, D_MODEL), BF16),
        ],
        compiler_params=pltpu.CompilerParams(
            dimension_semantics=("arbitrary",), vmem_limit_bytes=VMEM_LIMIT),
        name="merge",
    )(y_a, y_b, y_c, proj, h, w_br, w_out, g_ffn)


CONV_HALO = 8


def _ffn_kernel(xn_ref, h_ref, wg_ref, wv_ref, cw_ref, cb_ref, wd_ref, fg_ref, o_ref,
                acc_ref, gbuf_ref, carry_ref, *, n_f, final):
    si = pl.program_id(1)
    f = pl.program_id(2)
    x = xn_ref[...]
    gate = _dot(x, wg_ref[...])
    value = _dot(x, wv_ref[...])

    @pl.when(si == 0)
    def _():
        carry_ref[f] = jnp.zeros((CONV_HALO, FFN_TF), F32)

    gbuf_ref[0:CONV_HALO, :] = carry_ref[f]
    gbuf_ref[CONV_HALO:CONV_HALO + FFN_TM, :] = gate
    carry_ref[f] = gate[FFN_TM - CONV_HALO:FFN_TM, :]
    cw = cw_ref[...]
    conv = cb_ref[...] + gbuf_ref[CONV_HALO - 2:CONV_HALO - 2 + FFN_TM, :] * cw[0:1, :]
    conv = conv + gbuf_ref[CONV_HALO - 1:CONV_HALO - 1 + FFN_TM, :] * cw[1:2, :]
    conv = conv + gate * cw[2:3, :]
    hidden = (_gelu(conv) * value).astype(BF16)
    contrib = _dot(hidden, wd_ref[...])

    @pl.when(f == 0)
    def _():
        acc_ref[...] = contrib

    @pl.when(f > 0)
    def _():
        acc_ref[...] += contrib

    @pl.when(f == n_f - 1)
    def _():
        h_new = h_ref[...] + acc_ref[...]
        if final:
            h_new = _rms(h_new, fg_ref[...])
        o_ref[...] = h_new


def _ffn(xn, h, w_gate, w_value, conv_w, conv_b, w_down, final_g, batch, seq, final):
    t = h.shape[0]
    n_s = seq // FFN_TM
    n_f = D_FF // FFN_TF
    return pl.pallas_call(
        functools.partial(_ffn_kernel, n_f=n_f, final=final),
        grid=(batch, n_s, n_f),
        in_specs=[
            pl.BlockSpec((FFN_TM, D_MODEL), lambda b, s, f: (b * n_s + s, 0)),
            pl.BlockSpec((FFN_TM, D_MODEL), lambda b, s, f: (b * n_s + s, 0)),
            pl.BlockSpec((D_MODEL, FFN_TF), lambda b, s, f: (0, f)),
            pl.BlockSpec((D_MODEL, FFN_TF), lambda b, s, f: (0, f)),
            pl.BlockSpec((3, FFN_TF), lambda b, s, f: (0, f)),
            pl.BlockSpec((1, FFN_TF), lambda b, s, f: (0, f)),
            pl.BlockSpec((FFN_TF, D_MODEL), lambda b, s, f: (f, 0)),
            pl.BlockSpec((1, D_MODEL), lambda b, s, f: (0, 0)),
        ],
        out_specs=pl.BlockSpec((FFN_TM, D_MODEL), lambda b, s, f: (b * n_s + s, 0)),
        out_shape=jax.ShapeDtypeStruct((t, D_MODEL), F32),
        scratch_shapes=[
            pltpu.VMEM((FFN_TM, D_MODEL), F32),
            pltpu.VMEM((CONV_HALO + FFN_TM, FFN_TF), F32),
            pltpu.VMEM((n_f, CONV_HALO, FFN_TF), F32),
        ],
        compiler_params=pltpu.CompilerParams(
            dimension_semantics=("arbitrary", "arbitrary", "arbitrary"),
            vmem_limit_bytes=VMEM_LIMIT),
        name="ffn",
    )(xn, h, w_gate, w_value, conv_w, conv_b, w_down, final_g)


def kernel(x, norm_mix_g, w_in, fox_bias, sgu_ln_g, sgu_w, sgu_b, w_branch,
           w_out, norm_ffn_g, w_up, conv_w, conv_b, w_down, final_g):
    batch, seq, d_model = x.shape
    assert d_model == D_MODEL and seq % FFN_TM == 0 and seq % SGU_TM == 0
    assert seq % FOX_BK == 0 and FOX_BK % FOX_BQ == 0 and seq % ATT_BQ == 0
    assert (batch * seq) % PROJ_TM == 0 and (batch * seq) % MERGE_TM == 0
    t = batch * seq

    f_lo, f_hi = QKV_COLS, QKV_COLS + N_HEADS
    uv_hi = f_hi + UV_COLS
    w_main = jnp.concatenate(
        [w_in[:, :, :f_lo], w_in[:, :, uv_hi:], w_in[:, :, f_hi:uv_hi]], axis=-1).astype(BF16)
    w_f = jnp.pad(w_in[:, :, f_lo:f_hi], ((0, 0), (0, 0), (0, LANES - N_HEADS))).astype(BF16)
    bias_f = jnp.pad(fox_bias, ((0, 0), (0, LANES - N_HEADS)))
    b_full = jnp.repeat(jnp.swapaxes(sgu_b, 1, 2), SGU_WIDTH // SGU_GROUPS, axis=2)
    b_full = b_full.reshape(DEPTH, SGU_CHUNK, SGU_WIDTH)
    w_br = w_branch.astype(BF16)
    w_o = w_out.astype(BF16)
    w_gate = w_up[:, :, :D_FF].astype(BF16)
    w_value = w_up[:, :, D_FF:].astype(BF16)
    w_dn = w_down.astype(BF16)

    h = x.reshape(t, D_MODEL)
    for l in range(DEPTH):
        proj, f = _proj(h, norm_mix_g[l][None, :], w_main[l], w_f[l])
        c_col = _fgate(f, bias_f[l][None, :], batch, seq)
        c_row = jnp.swapaxes(c_col.reshape(batch, seq, LANES)[:, :, :N_HEADS], 1, 2)
        c_row = c_row.reshape(batch, N_HEADS, 1, seq)
        y_c = _sgu(proj, sgu_ln_g[l][None, :], sgu_w[l], b_full[l])
        y_a = _sb_attention(proj, batch, seq)
        y_b = _fox_attention(proj, c_col, c_row, batch, seq)
        h, xn = _merge(y_a, y_b, y_c, proj, h, w_br[l], w_o[l], norm_ffn_g[l][None, :])
        h = _ffn(xn, h, w_gate[l], w_value[l], conv_w[l], conv_b[l][None, :], w_dn[l],
                 final_g[None, :], batch, seq, final=(l == DEPTH - 1))
    return h.reshape(batch, seq, D_MODEL)
```

```python
import functools

import jax
import jax.numpy as jnp
from jax import lax
from jax.experimental import pallas as pl
from jax.experimental.pallas import tpu as pltpu

F32 = jnp.float32
BF16 = jnp.bfloat16

D_MODEL = 1024
DEPTH = 4
HEAD_DIM = 64
N_HEADS = 8
ATT_WIDTH = N_HEADS * HEAD_DIM
SGU_WIDTH = 512
SGU_GROUPS = 8
SGU_CHUNK = 128
STREAM_CHUNK = 64
N_BRANCH = 3
D_FF = 2816
EPS = 1e-6

LANES = 128
HEAD_PAIRS = ATT_WIDTH // LANES
QKV_COLS = 6 * ATT_WIDTH
UV_COLS = 2 * SGU_WIDTH
GATE_COLS = N_BRANCH * D_MODEL
MAIN_COLS = QKV_COLS + UV_COLS + GATE_COLS
GATE_BLOCK = QKV_COLS // GATE_COLS
UV_BLOCK = (QKV_COLS + GATE_COLS) // UV_COLS
assert GATE_BLOCK * GATE_COLS == QKV_COLS and UV_BLOCK * UV_COLS == QKV_COLS + GATE_COLS

VMEM_LIMIT = 56 * 1024 * 1024

PROJ_TM = 512
PROJ_TN = 1024
SGU_TM = 512
ATT_BQ = 128
ATT_BK = 128
FOX_BQ = 256
FOX_BK = 512
FOX_SUB = 128
MERGE_TM = 512
FFN_TM = 512
FFN_TF = 2816
NEG_BIG = -1e30


def _rms(x, g):
    ms = jnp.mean(x * x, axis=-1, keepdims=True)
    return x * lax.rsqrt(ms + EPS) * g


def _gelu(x):
    return 0.5 * x * (1.0 + lax.erf(x * (2.0 ** -0.5)))


def _softplus(z):
    return jnp.maximum(z, 0.0) + jnp.log(1.0 + jnp.exp(-jnp.abs(z)))


def _dot(a, b):
    return jnp.dot(a, b, preferred_element_type=F32)


def _dot_nt(a, b):
    return lax.dot_general(a, b, (((1,), (1,)), ((), ())), preferred_element_type=F32)


def _proj_kernel(x_ref, g_ref, w_ref, wf_ref, o_ref, f_ref):
    xn = _rms(x_ref[...], g_ref[...]).astype(BF16)
    f_ref[...] = _dot(xn, wf_ref[...])
    for c in range(MAIN_COLS // PROJ_TN):
        cols = slice(c * PROJ_TN, (c + 1) * PROJ_TN)
        o_ref[:, cols] = _dot(xn, w_ref[:, cols]).astype(o_ref.dtype)


def _proj(h, g, w_main, w_f):
    t = h.shape[0]
    return pl.pallas_call(
        _proj_kernel,
        grid=(t // PROJ_TM,),
        in_specs=[
            pl.BlockSpec((PROJ_TM, D_MODEL), lambda i: (i, 0)),
            pl.BlockSpec((1, D_MODEL), lambda i: (0, 0)),
            pl.BlockSpec((D_MODEL, MAIN_COLS), lambda i: (0, 0), pipeline_mode=pl.Buffered(1)),
            pl.BlockSpec((D_MODEL, LANES), lambda i: (0, 0)),
        ],
        out_specs=[
            pl.BlockSpec((PROJ_TM, MAIN_COLS), lambda i: (i, 0)),
            pl.BlockSpec((PROJ_TM, LANES), lambda i: (i, 0)),
        ],
        out_shape=[
            jax.ShapeDtypeStruct((t, MAIN_COLS), BF16),
            jax.ShapeDtypeStruct((t, LANES), F32),
        ],
        compiler_params=pltpu.CompilerParams(
            dimension_semantics=("arbitrary",),
            vmem_limit_bytes=VMEM_LIMIT),
        name="proj",
    )(h, g, w_main, w_f)


def _fgate_kernel(f_ref, b_ref, c_ref, *, seq):
    r = lax.broadcasted_iota(jnp.int32, (LANES, LANES), 0)
    c = lax.broadcasted_iota(jnp.int32, (LANES, LANES), 1)
    lower = (c <= r).astype(F32)
    bias = b_ref[...]

    def body(i, carry):
        rows = pl.ds(pl.multiple_of(i * LANES, LANES), LANES)
        x = f_ref[rows, :] + bias
        log_f = -_softplus(-x)
        cs = jnp.dot(lower, log_f, precision=lax.Precision.HIGHEST,
                     preferred_element_type=F32) + carry
        c_ref[rows, :] = cs
        return cs[LANES - 1:LANES, :]

    lax.fori_loop(0, seq // LANES, body, jnp.zeros((1, LANES), F32))


def _fgate(f, bias, batch, seq):
    return pl.pallas_call(
        functools.partial(_fgate_kernel, seq=seq),
        grid=(batch,),
        in_specs=[
            pl.BlockSpec((seq, LANES), lambda b: (b, 0)),
            pl.BlockSpec((1, LANES), lambda b: (0, 0)),
        ],
        out_specs=pl.BlockSpec((seq, LANES), lambda b: (b, 0)),
        out_shape=jax.ShapeDtypeStruct(f.shape, F32),
        compiler_params=pltpu.CompilerParams(
            dimension_semantics=("arbitrary",), vmem_limit_bytes=VMEM_LIMIT),
        name="fgate",
    )(f, bias)


def _sgu_kernel(uv_ref, lng_ref, ws_ref, b_ref, o_ref):
    uv = _gelu(uv_ref[...].astype(F32))
    u = uv[:, :SGU_WIDTH]
    v = uv[:, SGU_WIDTH:]
    mu = jnp.mean(v, axis=-1, keepdims=True)
    vc = v - mu
    var = jnp.mean(vc * vc, axis=-1, keepdims=True)
    vn = (vc * lax.rsqrt(var + EPS) * lng_ref[...]).astype(BF16)

    t_pos = lax.broadcasted_iota(jnp.int32, (SGU_CHUNK, SGU_CHUNK), 0)
    s_pos = lax.broadcasted_iota(jnp.int32, (SGU_CHUNK, SGU_CHUNK), 1)
    mask = (s_pos // STREAM_CHUNK) <= (t_pos // STREAM_CHUNK)
    lane = lax.broadcasted_iota(jnp.int32, (SGU_CHUNK, LANES), 1)
    first_group = lane < HEAD_DIM
    group_width = SGU_WIDTH // SGU_GROUPS
    assert 2 * group_width == LANES

    for p in range(SGU_GROUPS // 2):
        cols = slice(p * LANES, (p + 1) * LANES)
        w0 = jnp.where(mask, ws_ref[2 * p], 0.0).astype(BF16)
        w1 = jnp.where(mask, ws_ref[2 * p + 1], 0.0).astype(BF16)
        bias = b_ref[:, cols]
        for c in range(SGU_TM // SGU_CHUNK):
            rows = slice(c * SGU_CHUNK, (c + 1) * SGU_CHUNK)
            vp = vn[rows, cols]
            mixed = jnp.where(first_group, _dot(w0, vp), _dot(w1, vp)) + bias
            o_ref[rows, cols] = (u[rows, cols] * mixed).astype(o_ref.dtype)


def _sgu(proj, ln_g, w_s, b_full):
    t = proj.shape[0]
    return pl.pallas_call(
        _sgu_kernel,
        grid=(t // SGU_TM,),
        in_specs=[
            pl.BlockSpec((SGU_TM, UV_COLS), lambda i: (i, UV_BLOCK)),
            pl.BlockSpec((1, SGU_WIDTH), lambda i: (0, 0)),
            pl.BlockSpec((SGU_GROUPS, SGU_CHUNK, SGU_CHUNK), lambda i: (0, 0, 0)),
            pl.BlockSpec((SGU_CHUNK, SGU_WIDTH), lambda i: (0, 0)),
        ],
        out_specs=pl.BlockSpec((SGU_TM, SGU_WIDTH), lambda i: (i, 0)),
        out_shape=jax.ShapeDtypeStruct((t, SGU_WIDTH), BF16),
        compiler_params=pltpu.CompilerParams(
            dimension_semantics=("arbitrary",), vmem_limit_bytes=VMEM_LIMIT),
        name="sgu",
    )(proj, ln_g, w_s, b_full)


SB_STAY_FLOOR = -104.0


def _sb_kernel(q_ref, k_ref, v_ref, o_ref, qm_ref, stay_ref, y_ref):
    qi = pl.program_id(1)
    lane = lax.broadcasted_iota(jnp.int32, (ATT_BQ, LANES), 1)
    first_head = lane < HEAD_DIM
    row = lax.broadcasted_iota(jnp.int32, (ATT_BQ, ATT_BK), 0)
    col = lax.broadcasted_iota(jnp.int32, (ATT_BQ, ATT_BK), 1)
    strict = col < row
    ext_row = lax.broadcasted_iota(jnp.int32, (2 * ATT_BK, ATT_BK + LANES), 0) % ATT_BK
    ext_col = lax.broadcasted_iota(jnp.int32, (2 * ATT_BK, ATT_BK + LANES), 1)
    later_ext = jnp.where((ext_col >= ATT_BK) | (ext_row > ext_col), -1.0, 0.0).astype(BF16)

    q = q_ref[...] * (HEAD_DIM ** -0.5)
    zero = jnp.zeros((ATT_BQ, LANES), q.dtype)
    for p in range(HEAD_PAIRS):
        qp = q[:, p * LANES:(p + 1) * LANES]
        qm_ref[2 * p] = jnp.where(first_head, qp, zero)
        qm_ref[2 * p + 1] = jnp.where(first_head, zero, qp)

    def step(kb, diag):
        rows = pl.ds(pl.multiple_of(kb * ATT_BK, ATT_BK), ATT_BK)
        heads = range(N_HEADS)
        cols = [slice((h // 2) * LANES, (h // 2 + 1) * LANES) for h in heads]
        z = [_dot_nt(qm_ref[h], k_ref[rows, cols[h]]) for h in heads]
        sp = [_softplus(z[h]) for h in heads]
        if diag:
            sp = [jnp.where(strict, s, 0.0) for s in sp]
        hi = [s.astype(BF16) for s in sp]
        lo = [(sp[h] - hi[h].astype(F32)).astype(BF16) for h in heads]
        ext = [_dot(jnp.concatenate([hi[h], lo[h]], axis=1), later_ext) for h in heads]
        x = [z[h] - sp[h] + ext[h][:, :ATT_BK] for h in heads]
        if diag:
            stay = [e[:, ATT_BK:] for e in ext]
            w = [jnp.where(strict, jnp.exp(xh), 0.0) for xh in x]
        else:
            stay = [stay_ref[h] for h in heads]
            w = [jnp.exp(x[h] + stay[h]) for h in heads]
            stay = [stay[h] + ext[h][:, ATT_BK:] for h in heads]
        pv = [_dot(w[h].astype(BF16), v_ref[rows, cols[h]]) for h in heads]
        top = stay[0]
        for h in heads:
            stay_ref[h] = stay[h]
            top = jnp.maximum(top, stay[h])
        for p in range(HEAD_PAIRS):
            update = jnp.where(first_head, pv[2 * p], pv[2 * p + 1])
            if diag:
                y_ref[:, cols[2 * p]] = update
            else:
                y_ref[:, cols[2 * p]] += update
        return jnp.max(top)

    def more(carry):
        kb, top = carry
        return jnp.logical_and(kb >= 0, top > SB_STAY_FLOOR)

    lax.while_loop(more, lambda c: (c[0] - 1, step(c[0], False)), (qi - 1, step(qi, True)))
    o_ref[...] = y_ref[...].astype(o_ref.dtype)


def _sb_attention(proj, batch, seq):
    nq = seq // ATT_BQ
    return pl.pallas_call(
        _sb_kernel,
        grid=(batch, nq),
        in_specs=[
            pl.BlockSpec((ATT_BQ, ATT_WIDTH), lambda b, i: (b * nq + i, 0)),
            pl.BlockSpec((seq, ATT_WIDTH), lambda b, i: (b, 1)),
            pl.BlockSpec((seq, ATT_WIDTH), lambda b, i: (b, 2)),
        ],
        out_specs=pl.BlockSpec((ATT_BQ, ATT_WIDTH), lambda b, i: (b * nq + i, 0)),
        out_shape=jax.ShapeDtypeStruct((batch * seq, ATT_WIDTH), BF16),
        scratch_shapes=[
            pltpu.VMEM((N_HEADS, ATT_BQ, LANES), BF16),
            pltpu.VMEM((N_HEADS, ATT_BQ, LANES), F32),
            pltpu.VMEM((ATT_BQ, ATT_WIDTH), F32),
        ],
        compiler_params=pltpu.CompilerParams(
            dimension_semantics=("arbitrary", "arbitrary"),
            vmem_limit_bytes=VMEM_LIMIT),
        name="sb_attention",
    )(proj, proj, proj)


FOX_ZERO_LOGIT = -106.0
NORM_SLACK = 1.001


def _fox_kernel(cend_ref, q_ref, k_ref, v_ref, ccol_ref, crow0_ref, crow1_ref, o_ref,
                qm_ref, ct_ref, reach_ref, m_ref, acc_ref, p_ref, alpha_ref, knorm_ref, *, n_kb):
    b = pl.program_id(0)
    hp = pl.program_id(1)
    qi = pl.program_id(2)
    blocks_per_kb = FOX_BK // FOX_BQ
    kd = qi // blocks_per_kb
    q_off = (qi % blocks_per_kb) * FOX_BQ
    lane = lax.broadcasted_iota(jnp.int32, (FOX_BQ, LANES), 1)
    first_head = lane < HEAD_DIM
    head_lanes = (first_head, jnp.logical_not(first_head))

    @pl.when(qi == 0)
    def _():
        kk = k_ref[...].astype(F32)
        kk = kk * kk
        klane = lax.broadcasted_iota(jnp.int32, kk.shape, 1) < HEAD_DIM
        knorm_ref[0] = jnp.max(jnp.sqrt(jnp.sum(jnp.where(klane, kk, 0.0), axis=1, keepdims=True)))
        knorm_ref[1] = jnp.max(jnp.sqrt(jnp.sum(jnp.where(klane, 0.0, kk), axis=1, keepdims=True)))

    q = q_ref[...] * (HEAD_DIM ** -0.5)
    zero = jnp.zeros_like(q)
    q32 = q.astype(F32)
    q32 = q32 * q32
    ccol = ccol_ref[...]
    crow = (crow0_ref, crow1_ref)
    for hh in range(2):
        qm_ref[hh] = jnp.where(head_lanes[hh], q, zero)
        c_t = jnp.sum(jnp.where(lane == 2 * hp + hh, ccol, 0.0), axis=1, keepdims=True)
        qnorm = jnp.sqrt(jnp.sum(jnp.where(head_lanes[hh], q32, 0.0), axis=1, keepdims=True))
        ct_ref[hh] = jnp.broadcast_to(c_t, (FOX_BQ, LANES))
        reach_ref[hh] = jnp.broadcast_to(qnorm * knorm_ref[hh] * NORM_SLACK + c_t, (FOX_BQ, LANES))
    lane_tiles = FOX_BK // LANES

    chains = [(hh, slice(r * FOX_SUB, (r + 1) * FOX_SUB))
              for hh in range(2) for r in range(FOX_BQ // FOX_SUB)]
    ones = jnp.ones((FOX_BK, LANES), BF16)

    row = lax.broadcasted_iota(jnp.int32, (FOX_SUB, FOX_BK), 0)
    col = lax.broadcasted_iota(jnp.int32, (FOX_SUB, FOX_BK), 1)

    def flush(kb_prev):
        rows = pl.ds(pl.multiple_of(kb_prev * FOX_BK, FOX_BK), FOX_BK)
        v_ext = jnp.concatenate([v_ref[rows, :], ones], axis=1)
        pv = [_dot(p_ref[ci], v_ext) for ci in range(len(chains))]
        for ci, (hh, rs) in enumerate(chains):
            alpha = alpha_ref[ci]
            acc_ref[hh, rs, :] = jnp.concatenate([alpha, alpha], axis=1) * acc_ref[hh, rs, :] + pv[ci]

    def block(kb, diag):
        rows = pl.ds(pl.multiple_of(kb * FOX_BK, FOX_BK), FOX_BK)
        k = k_ref[rows, :]
        c_s = [crow[hh][0, 0, :, rows] for hh in range(2)]
        z = [_dot_nt(qm_ref[hh, rs, :], k)
             + (jnp.concatenate([ct_ref[hh, rs, :]] * lane_tiles, axis=1) - c_s[hh])
             for hh, rs in chains]
        if diag:
            z = [jnp.where(col <= row + (q_off + rs.start), zc, NEG_BIG)
                 for zc, (hh, rs) in zip(z, chains)]
        else:
            flush(kb + 1)
        worst = [None, None]
        for ci, (hh, rs) in enumerate(chains):
            zmax = jnp.broadcast_to(jnp.max(z[ci], axis=1, keepdims=True), (FOX_SUB, LANES))
            if diag:
                m_new = zmax
                alpha_ref[ci] = jnp.zeros_like(zmax)
            else:
                m_old = m_ref[hh, rs, :]
                m_new = jnp.maximum(m_old, zmax)
                alpha_ref[ci] = jnp.exp(m_old - m_new)
            p_ref[ci] = jnp.exp(z[ci] - jnp.concatenate([m_new] * lane_tiles, axis=1)).astype(BF16)
            m_ref[hh, rs, :] = m_new
            gap = reach_ref[hh, rs, :] - m_new
            worst[hh] = gap if worst[hh] is None else jnp.maximum(worst[hh], gap)
        live = None
        kb_next = jnp.maximum(kb - 1, 0)
        for hh in range(2):
            c_end = cend_ref[(b * N_HEADS + 2 * hp + hh) * n_kb + kb_next]
            alive = jnp.max(worst[hh]) - c_end > FOX_ZERO_LOGIT
            live = alive if live is None else jnp.logical_or(live, alive)
        return live

    acc_ref[...] = jnp.zeros(acc_ref.shape, F32)
    kb_last, _ = lax.while_loop(lambda c: jnp.logical_and(c[0] >= 0, c[1]),
                                lambda c: (c[0] - 1, block(c[0], False)),
                                (kd - 1, block(kd, True)))
    flush(kb_last + 1)
    y = [acc_ref[hh, :, :LANES] / acc_ref[hh, :, LANES:] for hh in range(2)]
    o_ref[...] = jnp.where(first_head, y[0], y[1]).astype(o_ref.dtype)


def _fox_attention(proj, c_col, c_row, batch, seq):
    nq = seq // FOX_BQ
    n_kb = seq // FOX_BK
    c_end = c_row[:, :, 0, FOX_BK - 1::FOX_BK].reshape(batch * N_HEADS * n_kb)
    return pl.pallas_call(
        functools.partial(_fox_kernel, n_kb=n_kb),
        grid=(batch, HEAD_PAIRS, nq),
        in_specs=[
            pl.BlockSpec(memory_space=pltpu.SMEM),
            pl.BlockSpec((FOX_BQ, LANES), lambda b, p, i: (b * nq + i, 3 * HEAD_PAIRS + p)),
            pl.BlockSpec((seq, LANES), lambda b, p, i: (b, 4 * HEAD_PAIRS + p)),
            pl.BlockSpec((seq, LANES), lambda b, p, i: (b, 5 * HEAD_PAIRS + p)),
            pl.BlockSpec((FOX_BQ, LANES), lambda b, p, i: (b * nq + i, 0)),
            pl.BlockSpec((1, 1, 1, seq), lambda b, p, i: (b, 2 * p, 0, 0)),
            pl.BlockSpec((1, 1, 1, seq), lambda b, p, i: (b, 2 * p + 1, 0, 0)),
        ],
        out_specs=pl.BlockSpec((FOX_BQ, LANES), lambda b, p, i: (b * nq + i, p)),
        out_shape=jax.ShapeDtypeStruct((batch * seq, ATT_WIDTH), BF16),
        scratch_shapes=[
            pltpu.VMEM((2, FOX_BQ, LANES), BF16),
            pltpu.VMEM((2, FOX_BQ, LANES), F32),
            pltpu.VMEM((2, FOX_BQ, LANES), F32),
            pltpu.VMEM((2, FOX_BQ, LANES), F32),
            pltpu.VMEM((2, FOX_BQ, 2 * LANES), F32),
            pltpu.VMEM((2 * FOX_BQ // FOX_SUB, FOX_SUB, FOX_BK), BF16),
            pltpu.VMEM((2 * FOX_BQ // FOX_SUB, FOX_SUB, LANES), F32),
            pltpu.SMEM((2,), F32),
        ],
        compiler_params=pltpu.CompilerParams(
            dimension_semantics=("arbitrary", "arbitrary", "arbitrary"),
            vmem_limit_bytes=VMEM_LIMIT),
        name="fox_attention",
    )(c_end, proj, proj, proj, c_col, c_row, c_row)


def _merge_kernel(ya_ref, yb_ref, yc_ref, gate_ref, h_ref, wbr_ref, wout_ref, g_ref,
                  h_out_ref, xn_out_ref):
    merged = None
    for i, y_ref in enumerate((ya_ref, yb_ref, yc_ref)):
        gate = gate_ref[:, i * D_MODEL:(i + 1) * D_MODEL].astype(F32)
        term = (1.0 / (1.0 + jnp.exp(-gate))) * _dot(y_ref[...], wbr_ref[i])
        merged = term if merged is None else merged + term
    h_new = h_ref[...] + _dot(merged.astype(BF16), wout_ref[...])
    h_out_ref[...] = h_new
    xn_out_ref[...] = _rms(h_new, g_ref[...]).astype(xn_out_ref.dtype)


def _merge(y_a, y_b, y_c, proj, h, w_br, w_out, g_ffn):
    t = h.shape[0]
    y_spec = pl.BlockSpec((MERGE_TM, ATT_WIDTH), lambda i: (i, 0))
    return pl.pallas_call(
        _merge_kernel,
        grid=(t // MERGE_TM,),
        in_specs=[
            y_spec, y_spec, y_spec,
            pl.BlockSpec((MERGE_TM, GATE_COLS), lambda i: (i, GATE_BLOCK)),
            pl.BlockSpec((MERGE_TM, D_MODEL), lambda i: (i, 0)),
            pl.BlockSpec((N_BRANCH, ATT_WIDTH, D_MODEL), lambda i: (0, 0, 0)),
            pl.BlockSpec((D_MODEL, D_MODEL), lambda i: (0, 0)),
            pl.BlockSpec((1, D_MODEL), lambda i: (0, 0)),
        ],
        out_specs=[
            pl.BlockSpec((MERGE_TM, D_MODEL), lambda i: (i, 0)),
            pl.BlockSpec((MERGE_TM, D_MODEL), lambda i: (i, 0)),
        ],
        out_shape=[
            jax.ShapeDtypeStruct((t, D_MODEL), F32),
            jax.ShapeDtypeStruct((t, D_MODEL), BF16),
        ],
        compiler_params=pltpu.CompilerParams(
            dimension_semantics=("arbitrary",), vmem_limit_bytes=VMEM_LIMIT),
        name="merge",
    )(y_a, y_b, y_c, proj, h, w_br, w_out, g_ffn)


CONV_HALO = 8
FFN_SUB = 256


def _ffn_kernel(xn_ref, h_ref, wg_ref, wv_ref, cw_ref, cb_ref, wd_ref, fg_ref, o_ref,
                acc_ref, carry_ref, *, n_f, final):
    si = pl.program_id(1)
    f = pl.program_id(2)

    @pl.when(si == 0)
    def _():
        carry_ref[f] = jnp.zeros((CONV_HALO, FFN_TF), F32)

    subs = [slice(r * FFN_SUB, (r + 1) * FFN_SUB) for r in range(FFN_TM // FFN_SUB)]
    w_gate = wg_ref[...]
    w_value = wv_ref[...]
    gate = [_dot(xn_ref[rs, :], w_gate) for rs in subs]
    value = [_dot(xn_ref[rs, :], w_value) for rs in subs]
    cw = cw_ref[...]
    bias = cb_ref[...]
    halo = carry_ref[f]
    contrib = []
    for g, val in zip(gate, value):
        head = jnp.concatenate([halo, g[:CONV_HALO, :]], axis=0)

        def delayed(k, g=g, head=head):
            first = pltpu.roll(head, k, 0)[CONV_HALO:, :]
            return jnp.concatenate([first, pltpu.roll(g, k, 0)[CONV_HALO:, :]], axis=0)

        conv = bias + delayed(2) * cw[0:1, :]
        conv = conv + delayed(1) * cw[1:2, :]
        conv = conv + g * cw[2:3, :]
        hidden = (_gelu(conv) * val).astype(BF16)
        contrib.append(_dot(hidden, wd_ref[...]))
        halo = g[FFN_SUB - CONV_HALO:, :]
    carry_ref[f] = halo
    contrib = jnp.concatenate(contrib, axis=0)

    @pl.when(f == 0)
    def _():
        acc_ref[...] = contrib

    @pl.when(f > 0)
    def _():
        acc_ref[...] += contrib

    @pl.when(f == n_f - 1)
    def _():
        h_new = h_ref[...] + acc_ref[...]
        if final:
            h_new = _rms(h_new, fg_ref[...])
        o_ref[...] = h_new


def _ffn(xn, h, w_gate, w_value, conv_w, conv_b, w_down, final_g, batch, seq, final):
    t = h.shape[0]
    n_s = seq // FFN_TM
    n_f = D_FF // FFN_TF
    weight_mode = pl.Buffered(1) if n_f == 1 else pl.Buffered(2)
    return pl.pallas_call(
        functools.partial(_ffn_kernel, n_f=n_f, final=final),
        grid=(batch, n_s, n_f),
        in_specs=[
            pl.BlockSpec((FFN_TM, D_MODEL), lambda b, s, f: (b * n_s + s, 0)),
            pl.BlockSpec((FFN_TM, D_MODEL), lambda b, s, f: (b * n_s + s, 0)),
            pl.BlockSpec((D_MODEL, FFN_TF), lambda b, s, f: (0, f), pipeline_mode=weight_mode),
            pl.BlockSpec((D_MODEL, FFN_TF), lambda b, s, f: (0, f), pipeline_mode=weight_mode),
            pl.BlockSpec((3, FFN_TF), lambda b, s, f: (0, f)),
            pl.BlockSpec((1, FFN_TF), lambda b, s, f: (0, f)),
            pl.BlockSpec((FFN_TF, D_MODEL), lambda b, s, f: (f, 0), pipeline_mode=weight_mode),
            pl.BlockSpec((1, D_MODEL), lambda b, s, f: (0, 0)),
        ],
        out_specs=pl.BlockSpec((FFN_TM, D_MODEL), lambda b, s, f: (b * n_s + s, 0)),
        out_shape=jax.ShapeDtypeStruct((t, D_MODEL), F32),
        scratch_shapes=[
            pltpu.VMEM((FFN_TM, D_MODEL), F32),
            pltpu.VMEM((n_f, CONV_HALO, FFN_TF), F32),
        ],
        compiler_params=pltpu.CompilerParams(
            dimension_semantics=("arbitrary", "arbitrary", "arbitrary"),
            vmem_limit_bytes=VMEM_LIMIT),
        name="ffn",
    )(xn, h, w_gate, w_value, conv_w, conv_b, w_down, final_g)


def kernel(x, norm_mix_g, w_in, fox_bias, sgu_ln_g, sgu_w, sgu_b, w_branch,
           w_out, norm_ffn_g, w_up, conv_w, conv_b, w_down, final_g):
    batch, seq, d_model = x.shape
    assert d_model == D_MODEL and seq % FFN_TM == 0 and seq % SGU_TM == 0
    assert seq % FOX_BK == 0 and FOX_BK % FOX_BQ == 0 and seq % ATT_BQ == 0
    assert (batch * seq) % PROJ_TM == 0 and (batch * seq) % MERGE_TM == 0
    t = batch * seq

    f_lo, f_hi = QKV_COLS, QKV_COLS + N_HEADS
    uv_hi = f_hi + UV_COLS
    w_main = jnp.concatenate(
        [w_in[:, :, :f_lo], w_in[:, :, uv_hi:], w_in[:, :, f_hi:uv_hi]], axis=-1).astype(BF16)
    w_f = jnp.pad(w_in[:, :, f_lo:f_hi], ((0, 0), (0, 0), (0, LANES - N_HEADS))).astype(BF16)
    bias_f = jnp.pad(fox_bias, ((0, 0), (0, LANES - N_HEADS)))
    b_full = jnp.repeat(jnp.swapaxes(sgu_b, 1, 2), SGU_WIDTH // SGU_GROUPS, axis=2)
    b_full = b_full.reshape(DEPTH, SGU_CHUNK, SGU_WIDTH)
    w_br = w_branch.astype(BF16)
    w_o = w_out.astype(BF16)
    w_gate = w_up[:, :, :D_FF].astype(BF16)
    w_value = w_up[:, :, D_FF:].astype(BF16)
    w_dn = w_down.astype(BF16)

    h = x.reshape(t, D_MODEL)
    for l in range(DEPTH):
        proj, f = _proj(h, norm_mix_g[l][None, :], w_main[l], w_f[l])
        c_col = _fgate(f, bias_f[l][None, :], batch, seq)
        c_row = jnp.swapaxes(c_col.reshape(batch, seq, LANES)[:, :, :N_HEADS], 1, 2)
        c_row = c_row.reshape(batch, N_HEADS, 1, seq)
        y_c = _sgu(proj, sgu_ln_g[l][None, :], sgu_w[l], b_full[l])
        y_a = _sb_attention(proj, batch, seq)
        y_b = _fox_attention(proj, c_col, c_row, batch, seq)
        h, xn = _merge(y_a, y_b, y_c, proj, h, w_br[l], w_o[l], norm_ffn_g[l][None, :])
        h = _ffn(xn, h, w_gate[l], w_value[l], conv_w[l], conv_b[l][None, :], w_dn[l],
                 final_g[None, :], batch, seq, final=(l == DEPTH - 1))
    return h.reshape(batch, seq, D_MODEL)
```

```python
import functools

import jax
import jax.numpy as jnp
from jax import lax
from jax.experimental import pallas as pl
from jax.experimental.pallas import tpu as pltpu

F32 = jnp.float32
BF16 = jnp.bfloat16

D_MODEL = 1024
DEPTH = 4
HEAD_DIM = 64
N_HEADS = 8
ATT_WIDTH = N_HEADS * HEAD_DIM
SGU_WIDTH = 512
SGU_GROUPS = 8
SGU_CHUNK = 128
STREAM_CHUNK = 64
N_BRANCH = 3
D_FF = 2816
EPS = 1e-6

LANES = 128
SUBLANES = 8
HEAD_PAIRS = ATT_WIDTH // LANES
QKV_COLS = 6 * ATT_WIDTH
UV_COLS = 2 * SGU_WIDTH
GATE_COLS = N_BRANCH * D_MODEL
MAIN_COLS = QKV_COLS + UV_COLS + GATE_COLS
GATE_BLOCK = QKV_COLS // GATE_COLS
UV_BLOCK = (QKV_COLS + GATE_COLS) // UV_COLS
assert GATE_BLOCK * GATE_COLS == QKV_COLS and UV_BLOCK * UV_COLS == QKV_COLS + GATE_COLS

VMEM_LIMIT = 56 * 1024 * 1024

PROJ_TM = 512
PROJ_TN = 1024
SGU_TM = 512
ATT_BQ = 128
ATT_BK = 128
FOX_BQ = 256
FOX_BK = 512
FOX_SUB = 128
MERGE_TM = 512
FFN_TM = 512
FFN_TF = 2816
NEG_BIG = -1e30


def _rms(x, g):
    ms = jnp.mean(x * x, axis=-1, keepdims=True)
    return x * lax.rsqrt(ms + EPS) * g


def _gelu(x):
    return 0.5 * x * (1.0 + lax.erf(x * (2.0 ** -0.5)))


def _softplus(z):
    return jnp.maximum(z, 0.0) + jnp.log(1.0 + jnp.exp(-jnp.abs(z)))


def _dot(a, b):
    return jnp.dot(a, b, preferred_element_type=F32)


def _dot_nt(a, b):
    return lax.dot_general(a, b, (((1,), (1,)), ((), ())), preferred_element_type=F32)


def _proj_kernel(x_ref, g_ref, w_ref, wf_ref, fb_ref, o_ref, c_ref, carry_ref, *, tiles_per_seq):
    xn = _rms(x_ref[...], g_ref[...]).astype(BF16)

    @pl.when(pl.program_id(0) % tiles_per_seq == 0)
    def _():
        carry_ref[...] = jnp.zeros(carry_ref.shape, F32)

    log_f = -_softplus(-(_dot(xn, wf_ref[...]) + fb_ref[...]))
    r = lax.broadcasted_iota(jnp.int32, (LANES, LANES), 0)
    c = lax.broadcasted_iota(jnp.int32, (LANES, LANES), 1)
    lower = (c <= r).astype(BF16)
    lower = jnp.concatenate([lower, lower, lower], axis=1)
    hi = log_f.astype(BF16)
    rest = log_f - hi.astype(F32)
    mid = rest.astype(BF16)
    lo = (rest - mid.astype(F32)).astype(BF16)
    local = []
    for i in range(PROJ_TM // LANES):
        rows = slice(i * LANES, (i + 1) * LANES)
        pieces = jnp.concatenate([hi[rows, :], mid[rows, :], lo[rows, :]], axis=0)
        local.append(_dot(lower, pieces))
    carry = carry_ref[0:1, :]
    for i, cs in enumerate(local):
        cs = cs + carry
        c_ref[i * LANES:(i + 1) * LANES, :] = cs
        carry = cs[LANES - 1:LANES, :]
    carry_ref[0:1, :] = carry

    for c in range(MAIN_COLS // PROJ_TN):
        cols = slice(c * PROJ_TN, (c + 1) * PROJ_TN)
        o_ref[:, cols] = _dot(xn, w_ref[:, cols]).astype(o_ref.dtype)


def _proj(h, g, w_main, w_f, bias_f, seq):
    t = h.shape[0]
    return pl.pallas_call(
        functools.partial(_proj_kernel, tiles_per_seq=seq // PROJ_TM),
        grid=(t // PROJ_TM,),
        in_specs=[
            pl.BlockSpec((PROJ_TM, D_MODEL), lambda i: (i, 0)),
            pl.BlockSpec((1, D_MODEL), lambda i: (0, 0)),
            pl.BlockSpec((D_MODEL, MAIN_COLS), lambda i: (0, 0), pipeline_mode=pl.Buffered(1)),
            pl.BlockSpec((D_MODEL, LANES), lambda i: (0, 0)),
            pl.BlockSpec((1, LANES), lambda i: (0, 0)),
        ],
        out_specs=[
            pl.BlockSpec((PROJ_TM, MAIN_COLS), lambda i: (i, 0)),
            pl.BlockSpec((PROJ_TM, LANES), lambda i: (i, 0)),
        ],
        out_shape=[
            jax.ShapeDtypeStruct((t, MAIN_COLS), BF16),
            jax.ShapeDtypeStruct((t, LANES), F32),
        ],
        scratch_shapes=[pltpu.VMEM((SUBLANES, LANES), F32)],
        compiler_params=pltpu.CompilerParams(
            dimension_semantics=("arbitrary",),
            vmem_limit_bytes=VMEM_LIMIT),
        name="proj",
    )(h, g, w_main, w_f, bias_f)


def _sgu_kernel(uv_ref, lng_ref, ws_ref, b_ref, o_ref):
    uv = _gelu(uv_ref[...].astype(F32))
    u = uv[:, :SGU_WIDTH]
    v = uv[:, SGU_WIDTH:]
    mu = jnp.mean(v, axis=-1, keepdims=True)
    vc = v - mu
    var = jnp.mean(vc * vc, axis=-1, keepdims=True)
    vn = (vc * lax.rsqrt(var + EPS) * lng_ref[...]).astype(BF16)

    t_pos = lax.broadcasted_iota(jnp.int32, (SGU_CHUNK, SGU_CHUNK), 0)
    s_pos = lax.broadcasted_iota(jnp.int32, (SGU_CHUNK, SGU_CHUNK), 1)
    mask = (s_pos // STREAM_CHUNK) <= (t_pos // STREAM_CHUNK)
    lane = lax.broadcasted_iota(jnp.int32, (SGU_CHUNK, LANES), 1)
    first_group = lane < HEAD_DIM
    group_width = SGU_WIDTH // SGU_GROUPS
    assert 2 * group_width == LANES

    for p in range(SGU_GROUPS // 2):
        cols = slice(p * LANES, (p + 1) * LANES)
        w0 = jnp.where(mask, ws_ref[2 * p], 0.0).astype(BF16)
        w1 = jnp.where(mask, ws_ref[2 * p + 1], 0.0).astype(BF16)
        bias = b_ref[:, cols]
        for c in range(SGU_TM // SGU_CHUNK):
            rows = slice(c * SGU_CHUNK, (c + 1) * SGU_CHUNK)
            vp = vn[rows, cols]
            mixed = jnp.where(first_group, _dot(w0, vp), _dot(w1, vp)) + bias
            o_ref[rows, cols] = (u[rows, cols] * mixed).astype(o_ref.dtype)


def _sgu(proj, ln_g, w_s, b_full):
    t = proj.shape[0]
    return pl.pallas_call(
        _sgu_kernel,
        grid=(t // SGU_TM,),
        in_specs=[
            pl.BlockSpec((SGU_TM, UV_COLS), lambda i: (i, UV_BLOCK)),
            pl.BlockSpec((1, SGU_WIDTH), lambda i: (0, 0)),
            pl.BlockSpec((SGU_GROUPS, SGU_CHUNK, SGU_CHUNK), lambda i: (0, 0, 0)),
            pl.BlockSpec((SGU_CHUNK, SGU_WIDTH), lambda i: (0, 0)),
        ],
        out_specs=pl.BlockSpec((SGU_TM, SGU_WIDTH), lambda i: (i, 0)),
        out_shape=jax.ShapeDtypeStruct((t, SGU_WIDTH), BF16),
        compiler_params=pltpu.CompilerParams(
            dimension_semantics=("arbitrary",), vmem_limit_bytes=VMEM_LIMIT),
        name="sgu",
    )(proj, ln_g, w_s, b_full)


SB_STAY_FLOOR = -104.0


def _sb_kernel(q_ref, k_ref, v_ref, o_ref, qm_ref, stay_ref, y_ref):
    qi = pl.program_id(1)
    lane = lax.broadcasted_iota(jnp.int32, (ATT_BQ, LANES), 1)
    first_head = lane < HEAD_DIM
    row = lax.broadcasted_iota(jnp.int32, (ATT_BQ, ATT_BK), 0)
    col = lax.broadcasted_iota(jnp.int32, (ATT_BQ, ATT_BK), 1)
    strict = col < row
    ext_row = lax.broadcasted_iota(jnp.int32, (2 * ATT_BK, ATT_BK + LANES), 0) % ATT_BK
    ext_col = lax.broadcasted_iota(jnp.int32, (2 * ATT_BK, ATT_BK + LANES), 1)
    later_ext = jnp.where((ext_col >= ATT_BK) | (ext_row > ext_col), -1.0, 0.0).astype(BF16)

    q = q_ref[...] * (HEAD_DIM ** -0.5)
    zero = jnp.zeros((ATT_BQ, LANES), q.dtype)
    for p in range(HEAD_PAIRS):
        qp = q[:, p * LANES:(p + 1) * LANES]
        qm_ref[2 * p] = jnp.where(first_head, qp, zero)
        qm_ref[2 * p + 1] = jnp.where(first_head, zero, qp)

    def step(kb, diag):
        rows = pl.ds(pl.multiple_of(kb * ATT_BK, ATT_BK), ATT_BK)
        heads = range(N_HEADS)
        cols = [slice((h // 2) * LANES, (h // 2 + 1) * LANES) for h in heads]
        z = [_dot_nt(qm_ref[h], k_ref[rows, cols[h]]) for h in heads]
        sp = [_softplus(z[h]) for h in heads]
        if diag:
            sp = [jnp.where(strict, s, 0.0) for s in sp]
        hi = [s.astype(BF16) for s in sp]
        lo = [(sp[h] - hi[h].astype(F32)).astype(BF16) for h in heads]
        ext = [_dot(jnp.concatenate([hi[h], lo[h]], axis=1), later_ext) for h in heads]
        x = [z[h] - sp[h] + ext[h][:, :ATT_BK] for h in heads]
        if diag:
            stay = [e[:, ATT_BK:] for e in ext]
            w = [jnp.where(strict, jnp.exp(xh), 0.0) for xh in x]
        else:
            stay = [stay_ref[h] for h in heads]
            w = [jnp.exp(x[h] + stay[h]) for h in heads]
            stay = [stay[h] + ext[h][:, ATT_BK:] for h in heads]
        pv = [_dot(w[h].astype(BF16), v_ref[rows, cols[h]]) for h in heads]
        top = stay[0]
        for h in heads:
            stay_ref[h] = stay[h]
            top = jnp.maximum(top, stay[h])
        for p in range(HEAD_PAIRS):
            update = jnp.where(first_head, pv[2 * p], pv[2 * p + 1])
            if diag:
                y_ref[:, cols[2 * p]] = update
            else:
                y_ref[:, cols[2 * p]] += update
        return jnp.max(top)

    def more(carry):
        kb, top = carry
        return jnp.logical_and(kb >= 0, top > SB_STAY_FLOOR)

    lax.while_loop(more, lambda c: (c[0] - 1, step(c[0], False)), (qi - 1, step(qi, True)))
    o_ref[...] = y_ref[...].astype(o_ref.dtype)


def _sb_attention(proj, batch, seq):
    nq = seq // ATT_BQ
    return pl.pallas_call(
        _sb_kernel,
        grid=(batch, nq),
        in_specs=[
            pl.BlockSpec((ATT_BQ, ATT_WIDTH), lambda b, i: (b * nq + i, 0)),
            pl.BlockSpec((seq, ATT_WIDTH), lambda b, i: (b, 1)),
            pl.BlockSpec((seq, ATT_WIDTH), lambda b, i: (b, 2)),
        ],
        out_specs=pl.BlockSpec((ATT_BQ, ATT_WIDTH), lambda b, i: (b * nq + i, 0)),
        out_shape=jax.ShapeDtypeStruct((batch * seq, ATT_WIDTH), BF16),
        scratch_shapes=[
            pltpu.VMEM((N_HEADS, ATT_BQ, LANES), BF16),
            pltpu.VMEM((N_HEADS, ATT_BQ, LANES), F32),
            pltpu.VMEM((ATT_BQ, ATT_WIDTH), F32),
        ],
        compiler_params=pltpu.CompilerParams(
            dimension_semantics=("arbitrary", "arbitrary"),
            vmem_limit_bytes=VMEM_LIMIT),
        name="sb_attention",
    )(proj, proj, proj)


FOX_ZERO_LOGIT = -106.0
NORM_SLACK = 1.001


def _fox_kernel(cend_ref, q_ref, k_ref, v_ref, ccol_ref, crow0_ref, crow1_ref, o_ref,
                qm_ref, ct_ref, reach_ref, m_ref, acc_ref, p_ref, alpha_ref, knorm_ref, *, n_kb):
    b = pl.program_id(0)
    hp = pl.program_id(1)
    qi = pl.program_id(2)
    blocks_per_kb = FOX_BK // FOX_BQ
    kd = qi // blocks_per_kb
    q_off = (qi % blocks_per_kb) * FOX_BQ
    lane = lax.broadcasted_iota(jnp.int32, (FOX_BQ, LANES), 1)
    first_head = lane < HEAD_DIM
    head_lanes = (first_head, jnp.logical_not(first_head))

    @pl.when(qi == 0)
    def _():
        kk = k_ref[...].astype(F32)
        kk = kk * kk
        klane = lax.broadcasted_iota(jnp.int32, kk.shape, 1) < HEAD_DIM
        knorm_ref[0] = jnp.max(jnp.sqrt(jnp.sum(jnp.where(klane, kk, 0.0), axis=1, keepdims=True)))
        knorm_ref[1] = jnp.max(jnp.sqrt(jnp.sum(jnp.where(klane, 0.0, kk), axis=1, keepdims=True)))

    q = q_ref[...] * (HEAD_DIM ** -0.5)
    zero = jnp.zeros_like(q)
    q32 = q.astype(F32)
    q32 = q32 * q32
    ccol = ccol_ref[...]
    crow = (crow0_ref, crow1_ref)
    for hh in range(2):
        qm_ref[hh] = jnp.where(head_lanes[hh], q, zero)
        c_t = jnp.sum(jnp.where(lane == 2 * hp + hh, ccol, 0.0), axis=1, keepdims=True)
        qnorm = jnp.sqrt(jnp.sum(jnp.where(head_lanes[hh], q32, 0.0), axis=1, keepdims=True))
        ct_ref[hh] = jnp.broadcast_to(c_t, (FOX_BQ, LANES))
        reach_ref[hh] = jnp.broadcast_to(qnorm * knorm_ref[hh] * NORM_SLACK + c_t, (FOX_BQ, LANES))
    lane_tiles = FOX_BK // LANES

    chains = [(hh, slice(r * FOX_SUB, (r + 1) * FOX_SUB))
              for hh in range(2) for r in range(FOX_BQ // FOX_SUB)]
    ones = jnp.ones((FOX_BK, LANES), BF16)

    row = lax.broadcasted_iota(jnp.int32, (FOX_SUB, FOX_BK), 0)
    col = lax.broadcasted_iota(jnp.int32, (FOX_SUB, FOX_BK), 1)

    def flush(kb_prev):
        rows = pl.ds(pl.multiple_of(kb_prev * FOX_BK, FOX_BK), FOX_BK)
        v_ext = jnp.concatenate([v_ref[rows, :], ones], axis=1)
        pv = [_dot(p_ref[ci], v_ext) for ci in range(len(chains))]
        for ci, (hh, rs) in enumerate(chains):
            alpha = alpha_ref[ci]
            acc_ref[hh, rs, :] = jnp.concatenate([alpha, alpha], axis=1) * acc_ref[hh, rs, :] + pv[ci]

    def block(kb, diag):
        rows = pl.ds(pl.multiple_of(kb * FOX_BK, FOX_BK), FOX_BK)
        k = k_ref[rows, :]
        c_s = [crow[hh][0, 0, :, rows] for hh in range(2)]
        z = [_dot_nt(qm_ref[hh, rs, :], k)
             + (jnp.concatenate([ct_ref[hh, rs, :]] * lane_tiles, axis=1) - c_s[hh])
             for hh, rs in chains]
        if diag:
            z = [jnp.where(col <= row + (q_off + rs.start), zc, NEG_BIG)
                 for zc, (hh, rs) in zip(z, chains)]
        else:
            flush(kb + 1)
        worst = [None, None]
        for ci, (hh, rs) in enumerate(chains):
            zmax = jnp.broadcast_to(jnp.max(z[ci], axis=1, keepdims=True), (FOX_SUB, LANES))
            if diag:
                m_new = zmax
                alpha_ref[ci] = jnp.zeros_like(zmax)
            else:
                m_old = m_ref[hh, rs, :]
                m_new = jnp.maximum(m_old, zmax)
                alpha_ref[ci] = jnp.exp(m_old - m_new)
            p_ref[ci] = jnp.exp(z[ci] - jnp.concatenate([m_new] * lane_tiles, axis=1)).astype(BF16)
            m_ref[hh, rs, :] = m_new
            gap = reach_ref[hh, rs, :] - m_new
            worst[hh] = gap if worst[hh] is None else jnp.maximum(worst[hh], gap)
        live = None
        kb_next = jnp.maximum(kb - 1, 0)
        for hh in range(2):
            c_end = cend_ref[(b * N_HEADS + 2 * hp + hh) * n_kb + kb_next]
            alive = jnp.max(worst[hh]) - c_end > FOX_ZERO_LOGIT
            live = alive if live is None else jnp.logical_or(live, alive)
        return live

    acc_ref[...] = jnp.zeros(acc_ref.shape, F32)
    kb_last, _ = lax.while_loop(lambda c: jnp.logical_and(c[0] >= 0, c[1]),
                                lambda c: (c[0] - 1, block(c[0], False)),
                                (kd - 1, block(kd, True)))
    flush(kb_last + 1)
    y = [acc_ref[hh, :, :LANES] / acc_ref[hh, :, LANES:] for hh in range(2)]
    o_ref[...] = jnp.where(first_head, y[0], y[1]).astype(o_ref.dtype)


def _fox_attention(proj, c_col, c_row, batch, seq):
    nq = seq // FOX_BQ
    n_kb = seq // FOX_BK
    c_end = c_row[:, :, 0, FOX_BK - 1::FOX_BK].reshape(batch * N_HEADS * n_kb)
    return pl.pallas_call(
        functools.partial(_fox_kernel, n_kb=n_kb),
        grid=(batch, HEAD_PAIRS, nq),
        in_specs=[
            pl.BlockSpec(memory_space=pltpu.SMEM),
            pl.BlockSpec((FOX_BQ, LANES), lambda b, p, i: (b * nq + i, 3 * HEAD_PAIRS + p)),
            pl.BlockSpec((seq, LANES), lambda b, p, i: (b, 4 * HEAD_PAIRS + p)),
            pl.BlockSpec((seq, LANES), lambda b, p, i: (b, 5 * HEAD_PAIRS + p)),
            pl.BlockSpec((FOX_BQ, LANES), lambda b, p, i: (b * nq + i, 0)),
            pl.BlockSpec((1, 1, 1, seq), lambda b, p, i: (b, 2 * p, 0, 0)),
            pl.BlockSpec((1, 1, 1, seq), lambda b, p, i: (b, 2 * p + 1, 0, 0)),
        ],
        out_specs=pl.BlockSpec((FOX_BQ, LANES), lambda b, p, i: (b * nq + i, p)),
        out_shape=jax.ShapeDtypeStruct((batch * seq, ATT_WIDTH), BF16),
        scratch_shapes=[
            pltpu.VMEM((2, FOX_BQ, LANES), BF16),
            pltpu.VMEM((2, FOX_BQ, LANES), F32),
            pltpu.VMEM((2, FOX_BQ, LANES), F32),
            pltpu.VMEM((2, FOX_BQ, LANES), F32),
            pltpu.VMEM((2, FOX_BQ, 2 * LANES), F32),
            pltpu.VMEM((2 * FOX_BQ // FOX_SUB, FOX_SUB, FOX_BK), BF16),
            pltpu.VMEM((2 * FOX_BQ // FOX_SUB, FOX_SUB, LANES), F32),
            pltpu.SMEM((2,), F32),
        ],
        compiler_params=pltpu.CompilerParams(
            dimension_semantics=("arbitrary", "arbitrary", "arbitrary"),
            vmem_limit_bytes=VMEM_LIMIT),
        name="fox_attention",
    )(c_end, proj, proj, proj, c_col, c_row, c_row)


def _merge_kernel(ya_ref, yb_ref, yc_ref, gate_ref, h_ref, wbr_ref, wout_ref, g_ref,
                  h_out_ref, xn_out_ref):
    merged = None
    for i, y_ref in enumerate((ya_ref, yb_ref, yc_ref)):
        gate = gate_ref[:, i * D_MODEL:(i + 1) * D_MODEL].astype(F32)
        term = (1.0 / (1.0 + jnp.exp(-gate))) * _dot(y_ref[...], wbr_ref[i])
        merged = term if merged is None else merged + term
    h_new = h_ref[...] + _dot(merged.astype(BF16), wout_ref[...])
    h_out_ref[...] = h_new
    xn_out_ref[...] = _rms(h_new, g_ref[...]).astype(xn_out_ref.dtype)


def _merge(y_a, y_b, y_c, proj, h, w_br, w_out, g_ffn):
    t = h.shape[0]
    y_spec = pl.BlockSpec((MERGE_TM, ATT_WIDTH), lambda i: (i, 0))
    return pl.pallas_call(
        _merge_kernel,
        grid=(t // MERGE_TM,),
        in_specs=[
            y_spec, y_spec, y_spec,
            pl.BlockSpec((MERGE_TM, GATE_COLS), lambda i: (i, GATE_BLOCK)),
            pl.BlockSpec((MERGE_TM, D_MODEL), lambda i: (i, 0)),
            pl.BlockSpec((N_BRANCH, ATT_WIDTH, D_MODEL), lambda i: (0, 0, 0)),
            pl.BlockSpec((D_MODEL, D_MODEL), lambda i: (0, 0)),
            pl.BlockSpec((1, D_MODEL), lambda i: (0, 0)),
        ],
        out_specs=[
            pl.BlockSpec((MERGE_TM, D_MODEL), lambda i: (i, 0)),
            pl.BlockSpec((MERGE_TM, D_MODEL), lambda i: (i, 0)),
        ],
        out_shape=[
            jax.ShapeDtypeStruct((t, D_MODEL), F32),
            jax.ShapeDtypeStruct((t, D_MODEL), BF16),
        ],
        compiler_params=pltpu.CompilerParams(
            dimension_semantics=("arbitrary",), vmem_limit_bytes=VMEM_LIMIT),
        name="merge",
    )(y_a, y_b, y_c, proj, h, w_br, w_out, g_ffn)


CONV_HALO = 8
FFN_SUB = 256


def _ffn_kernel(xn_ref, h_ref, wg_ref, wv_ref, cw_ref, cb_ref, wd_ref, fg_ref, o_ref,
                acc_ref, carry_ref, *, n_f, final):
    si = pl.program_id(1)
    f = pl.program_id(2)

    @pl.when(si == 0)
    def _():
        carry_ref[f] = jnp.zeros((CONV_HALO, FFN_TF), F32)

    subs = [slice(r * FFN_SUB, (r + 1) * FFN_SUB) for r in range(FFN_TM // FFN_SUB)]
    w_gate = wg_ref[...]
    w_value = wv_ref[...]
    gate = [_dot(xn_ref[rs, :], w_gate) for rs in subs]
    value = [_dot(xn_ref[rs, :], w_value) for rs in subs]
    cw = cw_ref[...]
    bias = cb_ref[...]
    halo = carry_ref[f]
    contrib = []
    for g, val in zip(gate, value):
        head = jnp.concatenate([halo, g[:CONV_HALO, :]], axis=0)

        def delayed(k, g=g, head=head):
            first = pltpu.roll(head, k, 0)[CONV_HALO:, :]
            return jnp.concatenate([first, pltpu.roll(g, k, 0)[CONV_HALO:, :]], axis=0)

        conv = bias + delayed(2) * cw[0:1, :]
        conv = conv + delayed(1) * cw[1:2, :]
        conv = conv + g * cw[2:3, :]
        hidden = (_gelu(conv) * val).astype(BF16)
        contrib.append(_dot(hidden, wd_ref[...]))
        halo = g[FFN_SUB - CONV_HALO:, :]
    carry_ref[f] = halo
    contrib = jnp.concatenate(contrib, axis=0)

    @pl.when(f == 0)
    def _():
        acc_ref[...] = contrib

    @pl.when(f > 0)
    def _():
        acc_ref[...] += contrib

    @pl.when(f == n_f - 1)
    def _():
        h_new = h_ref[...] + acc_ref[...]
        if final:
            h_new = _rms(h_new, fg_ref[...])
        o_ref[...] = h_new


def _ffn(xn, h, w_gate, w_value, conv_w, conv_b, w_down, final_g, batch, seq, final):
    t = h.shape[0]
    n_s = seq // FFN_TM
    n_f = D_FF // FFN_TF
    weight_mode = pl.Buffered(1) if n_f == 1 else pl.Buffered(2)
    return pl.pallas_call(
        functools.partial(_ffn_kernel, n_f=n_f, final=final),
        grid=(batch, n_s, n_f),
        in_specs=[
            pl.BlockSpec((FFN_TM, D_MODEL), lambda b, s, f: (b * n_s + s, 0)),
            pl.BlockSpec((FFN_TM, D_MODEL), lambda b, s, f: (b * n_s + s, 0)),
            pl.BlockSpec((D_MODEL, FFN_TF), lambda b, s, f: (0, f), pipeline_mode=weight_mode),
            pl.BlockSpec((D_MODEL, FFN_TF), lambda b, s, f: (0, f), pipeline_mode=weight_mode),
            pl.BlockSpec((3, FFN_TF), lambda b, s, f: (0, f)),
            pl.BlockSpec((1, FFN_TF), lambda b, s, f: (0, f)),
            pl.BlockSpec((FFN_TF, D_MODEL), lambda b, s, f: (f, 0), pipeline_mode=weight_mode),
            pl.BlockSpec((1, D_MODEL), lambda b, s, f: (0, 0)),
        ],
        out_specs=pl.BlockSpec((FFN_TM, D_MODEL), lambda b, s, f: (b * n_s + s, 0)),
        out_shape=jax.ShapeDtypeStruct((t, D_MODEL), F32),
        scratch_shapes=[
            pltpu.VMEM((FFN_TM, D_MODEL), F32),
            pltpu.VMEM((n_f, CONV_HALO, FFN_TF), F32),
        ],
        compiler_params=pltpu.CompilerParams(
            dimension_semantics=("arbitrary", "arbitrary", "arbitrary"),
            vmem_limit_bytes=VMEM_LIMIT),
        name="ffn",
    )(xn, h, w_gate, w_value, conv_w, conv_b, w_down, final_g)


def kernel(x, norm_mix_g, w_in, fox_bias, sgu_ln_g, sgu_w, sgu_b, w_branch,
           w_out, norm_ffn_g, w_up, conv_w, conv_b, w_down, final_g):
    batch, seq, d_model = x.shape
    assert d_model == D_MODEL and seq % FFN_TM == 0 and seq % SGU_TM == 0
    assert seq % FOX_BK == 0 and FOX_BK % FOX_BQ == 0 and seq % ATT_BQ == 0
    assert seq % PROJ_TM == 0 and (batch * seq) % MERGE_TM == 0
    t = batch * seq

    f_lo, f_hi = QKV_COLS, QKV_COLS + N_HEADS
    uv_hi = f_hi + UV_COLS
    w_main = jnp.concatenate(
        [w_in[:, :, :f_lo], w_in[:, :, uv_hi:], w_in[:, :, f_hi:uv_hi]], axis=-1).astype(BF16)
    w_f = jnp.pad(w_in[:, :, f_lo:f_hi], ((0, 0), (0, 0), (0, LANES - N_HEADS))).astype(BF16)
    bias_f = jnp.pad(fox_bias, ((0, 0), (0, LANES - N_HEADS)))
    b_full = jnp.repeat(jnp.swapaxes(sgu_b, 1, 2), SGU_WIDTH // SGU_GROUPS, axis=2)
    b_full = b_full.reshape(DEPTH, SGU_CHUNK, SGU_WIDTH)
    w_br = w_branch.astype(BF16)
    w_o = w_out.astype(BF16)
    w_gate = w_up[:, :, :D_FF].astype(BF16)
    w_value = w_up[:, :, D_FF:].astype(BF16)
    w_dn = w_down.astype(BF16)

    h = x.reshape(t, D_MODEL)
    for l in range(DEPTH):
        proj, c_col = _proj(h, norm_mix_g[l][None, :], w_main[l], w_f[l], bias_f[l][None, :], seq)
        c_row = jnp.swapaxes(c_col.reshape(batch, seq, LANES)[:, :, :N_HEADS], 1, 2)
        c_row = c_row.reshape(batch, N_HEADS, 1, seq)
        y_c = _sgu(proj, sgu_ln_g[l][None, :], sgu_w[l], b_full[l])
        y_a = _sb_attention(proj, batch, seq)
        y_b = _fox_attention(proj, c_col, c_row, batch, seq)
        h, xn = _merge(y_a, y_b, y_c, proj, h, w_br[l], w_o[l], norm_ffn_g[l][None, :])
        h = _ffn(xn, h, w_gate[l], w_value[l], conv_w[l], conv_b[l][None, :], w_dn[l],
                 final_g[None, :], batch, seq, final=(l == DEPTH - 1))
    return h.reshape(batch, seq, D_MODEL)
```
